```python
import math
import jax
import jax.numpy as jnp
from jax import lax
import numpy as np

D_MODEL = 1024
BATCH = 16
SEQ = 2048
DEPTH = 1
DEC_BATCH = 2
DEC_SEQ = 8192
PAST_LEN = 128

SSD_D_INNER = 2 * D_MODEL
SSD_HEAD_DIM = 64
SSD_N_HEADS = SSD_D_INNER // SSD_HEAD_DIM
SSD_N_GROUPS = 4
SSD_D_STATE = 128
SSD_XBC = SSD_D_INNER + 2 * SSD_N_GROUPS * SSD_D_STATE
SSD_CONV = 5
SSD_CHUNK = 64
DT_MIN = 1e-3
DT_MAX = 1e-1

GLA_N_HEADS = 4
GLA_DK = D_MODEL // 2
GLA_DV = D_MODEL
GLA_HEAD_K = GLA_DK // GLA_N_HEADS
GLA_HEAD_V = GLA_DV // GLA_N_HEADS
GLA_GATE_RANK = 16
GLA_GATE_NORM = 16.0
GLA_CHUNK = 64

D_FF = 2816
FFN_CONV = 3
EPS = 1e-6

IN_SPLITS = (SSD_D_INNER, SSD_XBC, 2 * SSD_N_HEADS, GLA_DK, GLA_DK, GLA_DV, GLA_DV, 2 * GLA_GATE_RANK, 2 * D_MODEL)
IN_PROJ_DIM = sum(IN_SPLITS)

kernel_name = "hybrid_bidir_ssd_gla_convffn"


def _rms(x):
    xf = x.astype(jnp.float32)
    return xf * lax.rsqrt(jnp.mean(xf * xf, axis=-1, keepdims=True) + EPS)


def rmsnorm(x, w):
    return (_rms(x) * w.astype(jnp.float32)).astype(x.dtype)


def _flip(a):
    return jnp.flip(a, axis=1)


def dwconv_centred(x, w, b):
    k_w, ch = w.shape
    pad = (k_w - 1) // 2
    y = lax.conv_general_dilated(x, w[:, None, :].astype(x.dtype), window_strides=(1,), padding=((pad, pad),),
                                 dimension_numbers=('NWC', 'WIO', 'NWC'), feature_group_count=ch)
    return y + b.astype(x.dtype)


def ssd_scan(x, dt, a_log, bmat, cmat):
    bsz, seqlen, n_heads, hd = x.shape
    n_groups, d_state = bmat.shape[2], bmat.shape[3]
    hpg = n_heads // n_groups
    q_len = SSD_CHUNK
    n_chunks = seqlen // q_len
    a = (-jnp.exp(a_log) * dt).reshape(bsz, n_chunks, q_len, n_groups, hpg)
    xd = (x * dt[..., None]).reshape(bsz, n_chunks, q_len, n_groups, hpg, hd)
    bm = bmat.reshape(bsz, n_chunks, q_len, n_groups, d_state)
    cm = cmat.reshape(bsz, n_chunks, q_len, n_groups, d_state)
    a_cs = jnp.cumsum(a, axis=2)
    a_t = jnp.moveaxis(a_cs, 2, -1)
    seg = a_t[..., :, None] - a_t[..., None, :]
    mask = jnp.tril(jnp.ones((q_len, q_len), dtype=bool))
    decay_in = jnp.exp(jnp.where(mask, seg, -jnp.inf))
    cb = jnp.einsum('bcign,bcjgn->bcgij', cm, bm)
    y_diag = jnp.einsum('bcgij,bcgeij,bcjgep->bcigep', cb, decay_in, xd)
    a_last = a_cs[:, :, -1:]
    decay_to_end = jnp.exp(a_last - a_cs)
    states = jnp.einsum('bcqgn,bcqge,bcqgep->bcgepn', bm, decay_to_end, xd)
    chunk_decay = jnp.exp(a_last[:, :, 0])

    def step(h, inp):
        s, d = inp
        return d[..., None, None] * h + s, h

    h0 = jnp.zeros((bsz, n_groups, hpg, hd, d_state), dtype=x.dtype)
    _, h_in = lax.scan(step, h0, (jnp.moveaxis(states, 1, 0), jnp.moveaxis(chunk_decay, 1, 0)))
    h_in = jnp.moveaxis(h_in, 0, 1)
    y_off = jnp.einsum('bcqgn,bcgepn,bcqge->bcqgep', cm, h_in, jnp.exp(a_cs))
    return (y_diag + y_off).reshape(bsz, seqlen, n_heads, hd)


def gla_scan(q, k, v, log_a):
    bsz, seqlen, n_heads, dk = q.shape
    dv = v.shape[-1]
    q_len = GLA_CHUNK
    n_chunks = seqlen // q_len
    q = q.reshape(bsz, n_chunks, q_len, n_heads, dk)
    k = k.reshape(bsz, n_chunks, q_len, n_heads, dk)
    v = v.reshape(bsz, n_chunks, q_len, n_heads, dv)
    b_cs = jnp.cumsum(log_a.reshape(bsz, n_chunks, q_len, n_heads, dk), axis=2)
    q_in = q * jnp.exp(b_cs)
    k_in = k * jnp.exp(-b_cs)
    att = jnp.einsum('bcihd,bcjhd->bchij', q_in, k_in)
    mask = jnp.tril(jnp.ones((q_len, q_len), dtype=bool))
    att = jnp.where(mask, att, 0.0)
    o_intra = jnp.einsum('bchij,bcjhv->bcihv', att, v)
    b_last = b_cs[:, :, -1:]
    k_dec = k * jnp.exp(b_last - b_cs)
    states = jnp.einsum('bcqhd,bcqhv->bchdv', k_dec, v)
    chunk_decay = jnp.exp(b_last[:, :, 0])

    def step(h, inp):
        s, d = inp
        return d[..., None] * h + s, h

    h0 = jnp.zeros((bsz, n_heads, dk, dv), dtype=q.dtype)
    _, h_in = lax.scan(step, h0, (jnp.moveaxis(states, 1, 0), jnp.moveaxis(chunk_decay, 1, 0)))
    h_in = jnp.moveaxis(h_in, 0, 1)
    o_inter = jnp.einsum('bcqhd,bchdv->bcqhv', q_in, h_in)
    return (o_intra + o_inter).reshape(bsz, seqlen, n_heads, dv)


def encoder_layer(x, norm1_w, w_in, ssd_conv_w, ssd_conv_b, ssd_a_log, ssd_dt_bias, ssd_d, ssd_norm_w, w_ssd_out,
                  gla_gate_w2, gla_gate_b, gla_norm_w, w_gla_out, w_o, norm2_w, w_ffn_up, ffn_conv_w, ffn_conv_b,
                  w_ffn_down):
    f32 = jnp.float32
    bsz, seqlen, _ = x.shape
    h = rmsnorm(x, norm1_w)
    proj = h @ w_in
    offs = [int(o) for o in np.cumsum(IN_SPLITS)[:-1]]
    z, xbc, dt_raw, q, k, v, g, gk, gates = jnp.split(proj, offs, axis=-1)

    xbc = jax.nn.silu(dwconv_centred(xbc, ssd_conv_w, ssd_conv_b)).astype(f32)
    xs, bm, cm = jnp.split(xbc, [SSD_D_INNER, SSD_D_INNER + SSD_N_GROUPS * SSD_D_STATE], axis=-1)
    xs = xs.reshape(bsz, seqlen, SSD_N_HEADS, SSD_HEAD_DIM)
    bm = bm.reshape(bsz, seqlen, SSD_N_GROUPS, SSD_D_STATE)
    cm = cm.reshape(bsz, seqlen, SSD_N_GROUPS, SSD_D_STATE)
    dt = jax.nn.softplus(dt_raw.astype(f32).reshape(bsz, seqlen, 2, SSD_N_HEADS) + ssd_dt_bias.astype(f32))
    a_log = ssd_a_log.astype(f32)
    y_f = ssd_scan(xs, dt[:, :, 0], a_log[0], bm, cm)
    y_b = _flip(ssd_scan(_flip(xs), _flip(dt[:, :, 1]), a_log[1], _flip(bm), _flip(cm)))
    y = y_f + y_b + ssd_d.astype(f32)[:, None] * xs
    y = y.reshape(bsz, seqlen, SSD_D_INNER) * jax.nn.silu(z.astype(f32))
    y = _rms(y.reshape(bsz, seqlen, SSD_N_GROUPS, -1)).reshape(bsz, seqlen, SSD_D_INNER) * ssd_norm_w.astype(f32)
    branch_a = y.astype(x.dtype) @ w_ssd_out

    q = q.astype(f32).reshape(bsz, seqlen, GLA_N_HEADS, GLA_HEAD_K) * (GLA_HEAD_K ** -0.5)
    k = k.astype(f32).reshape(bsz, seqlen, GLA_N_HEADS, GLA_HEAD_K)
    v = v.astype(f32).reshape(bsz, seqlen, GLA_N_HEADS, GLA_HEAD_V)
    gk = gk.astype(f32).reshape(bsz, seqlen, 2, GLA_GATE_RANK)
    log_a = jax.nn.log_sigmoid(jnp.einsum('blzr,zrk->blzk', gk, gla_gate_w2.astype(f32)) + gla_gate_b.astype(f32)) / GLA_GATE_NORM
    log_a = log_a.reshape(bsz, seqlen, 2, GLA_N_HEADS, GLA_HEAD_K)
    o = gla_scan(q, k, v, log_a[:, :, 0]) + _flip(gla_scan(_flip(q), _flip(k), _flip(v), _flip(log_a[:, :, 1])))
    o = _rms(o).reshape(bsz, seqlen, GLA_DV) * gla_norm_w.astype(f32) * jax.nn.silu(g.astype(f32))
    branch_b = o.astype(x.dtype) @ w_gla_out

    gate_a, gate_b = jnp.split(gates, 2, axis=-1)
    mixed = jax.nn.sigmoid(gate_a) * branch_a + jax.nn.sigmoid(gate_b) * branch_b
    x = x + mixed @ w_o

    h = rmsnorm(x, norm2_w)
    u = dwconv_centred(h @ w_ffn_up, ffn_conv_w, ffn_conv_b)
    u_gate, u_val = jnp.split(u, 2, axis=-1)
    x = x + (jax.nn.silu(u_gate) * u_val) @ w_ffn_down
    return x


def trunk(x, norm1_w, w_in, ssd_conv_w, ssd_conv_b, ssd_a_log, ssd_dt_bias, ssd_d, ssd_norm_w, w_ssd_out,
          gla_gate_w2, gla_gate_b, gla_norm_w, w_gla_out, w_o, norm2_w, w_ffn_up, ffn_conv_w, ffn_conv_b,
          w_ffn_down, final_norm_w):
    for i in range(DEPTH):
        x = encoder_layer(x, norm1_w[i], w_in[i], ssd_conv_w[i], ssd_conv_b[i], ssd_a_log[i], ssd_dt_bias[i],
                          ssd_d[i], ssd_norm_w[i], w_ssd_out[i], gla_gate_w2[i], gla_gate_b[i], gla_norm_w[i],
                          w_gla_out[i], w_o[i], norm2_w[i], w_ffn_up[i], ffn_conv_w[i], ffn_conv_b[i], w_ffn_down[i])
    return rmsnorm(x, final_norm_w)


def setup_inputs(seed: int = 0) -> dict:
    key = jax.random.key(seed)
    ks = jax.random.split(key, 24)
    nrm = jax.random.normal
    f32 = jnp.float32
    u_dt = jax.random.uniform(ks[7], (DEPTH, 2, SSD_N_HEADS), dtype=f32)
    dt0 = jnp.exp(u_dt * (math.log(DT_MAX) - math.log(DT_MIN)) + math.log(DT_MIN))
    return {
        "x_prompt": nrm(ks[0], (BATCH, SEQ, D_MODEL), f32),
        "x_sample": nrm(ks[1], (DEC_BATCH, DEC_SEQ, D_MODEL), f32),
        "norm1_w": 1.0 + 0.02 * nrm(ks[2], (DEPTH, D_MODEL), f32),
        "w_in": nrm(ks[3], (DEPTH, D_MODEL, IN_PROJ_DIM), f32) * D_MODEL ** -0.5,
        "ssd_conv_w": nrm(ks[4], (DEPTH, SSD_CONV, SSD_XBC), f32) * SSD_CONV ** -0.5,
        "ssd_conv_b": 0.02 * nrm(ks[5], (DEPTH, SSD_XBC), f32),
        "ssd_a_log": jnp.log(jax.random.uniform(ks[6], (DEPTH, 2, SSD_N_HEADS), f32, minval=1.0, maxval=16.0)),
        "ssd_dt_bias": dt0 + jnp.log(-jnp.expm1(-dt0)),
        "ssd_d": 1.0 + 0.1 * nrm(ks[8], (DEPTH, SSD_N_HEADS), f32),
        "ssd_norm_w": 1.0 + 0.02 * nrm(ks[9], (DEPTH, SSD_D_INNER), f32),
        "w_ssd_out": nrm(ks[10], (DEPTH, SSD_D_INNER, D_MODEL), f32) * SSD_D_INNER ** -0.5,
        "gla_gate_w2": nrm(ks[11], (DEPTH, 2, GLA_GATE_RANK, GLA_DK), f32) * GLA_GATE_RANK ** -0.5,
        "gla_gate_b": 0.1 * nrm(ks[12], (DEPTH, 2, GLA_DK), f32),
        "gla_norm_w": 1.0 + 0.02 * nrm(ks[13], (DEPTH, GLA_DV), f32),
        "w_gla_out": nrm(ks[14], (DEPTH, GLA_DV, D_MODEL), f32) * GLA_DV ** -0.5,
        "w_o": nrm(ks[15], (DEPTH, D_MODEL, D_MODEL), f32) * D_MODEL ** -0.5,
        "norm2_w": 1.0 + 0.02 * nrm(ks[16], (DEPTH, D_MODEL), f32),
        "w_ffn_up": nrm(ks[17], (DEPTH, D_MODEL, 2 * D_FF), f32) * D_MODEL ** -0.5,
        "ffn_conv_w": nrm(ks[18], (DEPTH, FFN_CONV, 2 * D_FF), f32) * FFN_CONV ** -0.5,
        "ffn_conv_b": 0.02 * nrm(ks[19], (DEPTH, 2 * D_FF), f32),
        "w_ffn_down": nrm(ks[20], (DEPTH, D_FF, D_MODEL), f32) * D_FF ** -0.5,
        "final_norm_w": 1.0 + 0.02 * nrm(ks[21], (D_MODEL,), f32),
    }


def reference(x_prompt, x_sample, norm1_w, w_in, ssd_conv_w, ssd_conv_b, ssd_a_log, ssd_dt_bias, ssd_d, ssd_norm_w,
              w_ssd_out, gla_gate_w2, gla_gate_b, gla_norm_w, w_gla_out, w_o, norm2_w, w_ffn_up, ffn_conv_w,
              ffn_conv_b, w_ffn_down, final_norm_w):
    y_prompt = trunk(x_prompt, norm1_w, w_in, ssd_conv_w, ssd_conv_b, ssd_a_log, ssd_dt_bias, ssd_d, ssd_norm_w,
                     w_ssd_out, gla_gate_w2, gla_gate_b, gla_norm_w, w_gla_out, w_o, norm2_w, w_ffn_up, ffn_conv_w,
                     ffn_conv_b, w_ffn_down, final_norm_w)
    y_sample = trunk(x_sample, norm1_w, w_in, ssd_conv_w, ssd_conv_b, ssd_a_log, ssd_dt_bias, ssd_d, ssd_norm_w,
                     w_ssd_out, gla_gate_w2, gla_gate_b, gla_norm_w, w_gla_out, w_o, norm2_w, w_ffn_up, ffn_conv_w,
                     ffn_conv_b, w_ffn_down, final_norm_w)
    return (y_prompt, y_sample)
```

```python
import functools

import numpy as np
import jax
import jax.numpy as jnp
from jax import lax
from jax.experimental import pallas as pl
from jax.experimental.pallas import tpu as pltpu

F32 = jnp.float32
BF16 = jnp.bfloat16

D_MODEL = 1024
SSD_D_INNER = 2048
SSD_HEAD_DIM = 64
SSD_N_HEADS = 32
SSD_N_GROUPS = 4
SSD_D_STATE = 128
SSD_GROUP_W = SSD_D_INNER // SSD_N_GROUPS
SSD_BC_W = SSD_N_GROUPS * SSD_D_STATE
SSD_XBC = SSD_D_INNER + 2 * SSD_BC_W
SSD_CONV = 5
GLA_N_HEADS = 4
GLA_DK = 512
GLA_DV = 1024
GLA_HEAD_K = 128
GLA_HEAD_V = 256
GLA_GATE_RANK = 16
GLA_GATE_NORM = 16.0
GLA_CHUNK = 64
D_FF = 2816
EPS = 1e-6

LANES = 128
SUBLANES = 8
VMEM_LIMIT = 56 * 1024 * 1024

BIG_W = SSD_XBC + 2 * GLA_DK + 2 * GLA_DV + SSD_D_INNER + 2 * D_MODEL
SMALL_W = LANES
GK_OFF = 2 * SSD_N_HEADS

SSD_TILE = 256
SSD_Q = 128
GLA_TILE = 256
HALO = SUBLANES
SSD_HALO = 2 * SUBLANES
FFN_TILE = 512
FFN_F = 1408
PROJ_TM = 1024
PROJ_TN = 1024


def _dot(a, b):
    return jnp.dot(a, b, preferred_element_type=F32)


def _dot_tn(a, b):
    return lax.dot_general(a, b, (((0,), (0,)), ((), ())), preferred_element_type=F32)


def _dot_nt(a, b):
    return lax.dot_general(a, b, (((1,), (1,)), ((), ())), preferred_element_type=F32)


def _split_bf16(x, n):
    parts = []
    r = x
    for i in range(n):
        p = r.astype(BF16)
        parts.append(p)
        if i + 1 < n:
            r = r - p.astype(F32)
    return parts


def _dot_exact_lhs(m_bf16, x, n=3):
    acc = None
    for p in _split_bf16(x, n):
        t = _dot(m_bf16, p)
        acc = t if acc is None else acc + t
    return acc


def _sigmoid(x):
    return 1.0 / (1.0 + jnp.exp(-x))


def _silu(x):
    return x * _sigmoid(x)


def _softplus(x):
    return jnp.maximum(x, 0.0) + jnp.log1p(jnp.exp(-jnp.abs(x)))


def _cparams(n_axes):
    return pltpu.CompilerParams(dimension_semantics=("arbitrary",) * n_axes,
                                vmem_limit_bytes=VMEM_LIMIT)


def _proj_kernel(x_ref, nw_ref, w_ref, o_ref, h_ref):
    @pl.when(pl.program_id(1) == 0)
    def _():
        x = x_ref[...]
        ms = jnp.mean(x * x, axis=-1, keepdims=True)
        h_ref[...] = (x * lax.rsqrt(ms + EPS) * nw_ref[...]).astype(BF16)

    o_ref[...] = _dot(h_ref[...], w_ref[...]).astype(o_ref.dtype)


def _norm_proj(x, nw, w, out_dtype, tn):
    nt, d = x.shape
    n = w.shape[1]
    tm = min(PROJ_TM, nt)
    return pl.pallas_call(
        _proj_kernel,
        grid=(nt // tm, n // tn),
        in_specs=[pl.BlockSpec((tm, d), lambda i, j: (i, 0)),
                  pl.BlockSpec((1, d), lambda i, j: (0, 0)),
                  pl.BlockSpec((d, tn), lambda i, j: (0, j))],
        out_specs=pl.BlockSpec((tm, tn), lambda i, j: (i, j)),
        out_shape=jax.ShapeDtypeStruct((nt, n), out_dtype),
        scratch_shapes=[pltpu.VMEM((tm, d), BF16)],
        compiler_params=_cparams(2),
        name="norm_proj",
    )(x, nw, w)


def _fill_ext(ext_ref, prev_ref, cur_ref, next_ref, first, last, tile, width):
    halo = SSD_HALO
    prev = prev_ref[:, :width].astype(F32)
    nxt = next_ref[:, :width].astype(F32)
    ext_ref[0:halo, :] = jnp.where(first, 0.0, prev)
    ext_ref[halo:halo + tile, :] = cur_ref[:, :width].astype(F32)
    ext_ref[halo + tile:halo + tile + halo, :] = jnp.where(last, 0.0, nxt)


def _conv_taps(ext_ref, w_ref, b_ref, tile, lo, hi, n_taps):
    pad = (n_taps - 1) // 2
    acc = None
    for k in range(n_taps):
        xk = ext_ref[SSD_HALO - pad + k:SSD_HALO - pad + k + tile, lo:hi]
        t = xk * w_ref[k:k + 1, lo:hi]
        acc = t if acc is None else acc + t
    return acc + b_ref[:, lo:hi]


def _ssd_small(small_ref, bias_ref, nega_ref, tri_ref, ones_ref):
    s = small_ref[...]
    dt = _softplus(s + bias_ref[...])
    a = dt * nega_ref[...]
    p = _dot_exact_lhs(tri_ref[...], a)
    tot = _dot_exact_lhs(ones_ref[...], a)
    col = lax.broadcasted_iota(jnp.int32, s.shape, 1)
    fwd = col < SSD_N_HEADS
    r = tot - p + a
    pr = jnp.where(fwd, p, r)
    w = dt * jnp.exp(jnp.where(fwd, tot - p, p - a))
    rowarg = jnp.log(dt) - pr
    dsum = pltpu.roll(dt, 2 * SSD_N_HEADS, 1) + pltpu.roll(dt, SSD_N_HEADS, 1)
    diag_cols = (col >= 2 * SSD_N_HEADS) & (col < 3 * SSD_N_HEADS)
    rowarg = jnp.where(diag_cols, jnp.log(dsum), rowarg)
    return pr, w, rowarg


def _expand2(x, e2_ref):
    hi, lo = _split_bf16(x, 2)
    return _dot(jnp.concatenate([hi, lo], axis=1), e2_ref[...])


def _ssd_bwd_state_step(sb, xs_c, b_c, wb_c, decb_row):
    out = []
    for g in range(SSD_N_GROUPS):
        gl = slice(g * SSD_GROUP_W, (g + 1) * SSD_GROUP_W)
        xd = (xs_c[:, gl] * wb_c[:, gl]).astype(BF16)
        bg = b_c[:, g * SSD_D_STATE:(g + 1) * SSD_D_STATE].astype(BF16)
        out.append(decb_row[:, gl] * sb[g] + _dot_tn(bg, xd))
    return out


def _ssd_pre_kernel(prev_ref, cur_ref, next_ref, small_ref, convw_ref, convb_ref, bias_ref, nega_ref,
                    tri_ref, ones_ref, eb2_ref, sbin_ref,
                    ext_ref, xs_ref, wb_ref, exb_ref, sb_ref, *, n_tiles):
    t = pl.program_id(1)
    tile = SSD_TILE
    nc = tile // SSD_Q
    width = SSD_D_INNER + SSD_BC_W

    @pl.when(t == 0)
    def _():
        sb_ref[...] = jnp.zeros_like(sb_ref)

    sbin_ref[0, 0] = sb_ref[...]

    _fill_ext(ext_ref, prev_ref, cur_ref, next_ref, t == n_tiles - 1, t == 0, tile, width)
    xs_ref[...] = _silu(_conv_taps(ext_ref, convw_ref, convb_ref, tile, 0, width, SSD_CONV))

    pr, w, _ = _ssd_small(small_ref, bias_ref, nega_ref, tri_ref, ones_ref)
    wb_ref[...] = _dot(w.astype(BF16), eb2_ref[0:LANES, :])
    exb_ref[...] = _expand2(jnp.exp(pr), eb2_ref)

    def body(k, carry):
        c = nc - 1 - k
        rows = pl.ds(pl.multiple_of(c * SSD_Q, SSD_Q), SSD_Q)
        xs_c = xs_ref[rows, 0:SSD_D_INNER]
        b_c = xs_ref[rows, SSD_D_INNER:width]
        decb = exb_ref[pl.ds(pl.multiple_of(c * SSD_Q, SSD_Q), 1), :]
        sb = [sb_ref[g] for g in range(SSD_N_GROUPS)]
        new = _ssd_bwd_state_step(sb, xs_c, b_c, wb_ref[rows, :], decb)
        for g in range(SSD_N_GROUPS):
            sb_ref[g] = new[g]
        return carry

    lax.fori_loop(0, nc, body, 0)


def _ssd_main_kernel(prev_ref, cur_ref, next_ref, z_ref, small_ref, sbin_ref,
                     convw_ref, convb_ref, bias_ref, nega_ref, tri_ref, ones_ref, ef2_ref, eb2_ref,
                     drow_ref, normw_ref, wout_ref, o_ref,
                     ext_ref, xs_ref, pr_ref, rowarg_ref, wf_ref, wb_ref, exf_ref, exb_ref,
                     sf_ref, sbc_ref, y_ref, *, n_tiles):
    t = pl.program_id(1)
    tile = SSD_TILE
    q = SSD_Q
    nc = tile // q

    @pl.when(t == 0)
    def _():
        sf_ref[...] = jnp.zeros_like(sf_ref)

    _fill_ext(ext_ref, prev_ref, cur_ref, next_ref, t == 0, t == n_tiles - 1, tile, SSD_XBC)
    xs_ref[...] = _silu(_conv_taps(ext_ref, convw_ref, convb_ref, tile, 0, SSD_XBC, SSD_CONV))

    pr, w, rowarg = _ssd_small(small_ref, bias_ref, nega_ref, tri_ref, ones_ref)
    pr_ref[...] = pr
    rowarg_ref[...] = rowarg
    wbf = w.astype(BF16)
    wf_ref[...] = _dot(wbf, ef2_ref[0:LANES, :])
    wb_ref[...] = _dot(wbf, eb2_ref[0:LANES, :])
    ex = jnp.exp(pr)
    exf_ref[...] = _expand2(ex, ef2_ref)
    exb_ref[...] = _expand2(ex, eb2_ref)

    sbc_ref[nc - 1] = sbin_ref[0, 0]

    def bwd_body(k, carry):
        c = nc - 1 - k
        rows = pl.ds(pl.multiple_of(c * q, q), q)
        xs_c = xs_ref[rows, 0:SSD_D_INNER]
        b_c = xs_ref[rows, SSD_D_INNER:SSD_D_INNER + SSD_BC_W]
        decb = exb_ref[pl.ds(pl.multiple_of(c * q, q), 1), :]
        sb = [sbc_ref[c, g] for g in range(SSD_N_GROUPS)]
        new = _ssd_bwd_state_step(sb, xs_c, b_c, wb_ref[rows, :], decb)
        for g in range(SSD_N_GROUPS):
            sbc_ref[c - 1, g] = new[g]
        return carry

    lax.fori_loop(0, nc - 1, bwd_body, 0)

    ii = lax.broadcasted_iota(jnp.int32, (q, q), 0)
    jj = lax.broadcasted_iota(jnp.int32, (q, q), 1)
    lower = ii > jj
    upper = ii < jj
    lane = lax.broadcasted_iota(jnp.int32, (q, LANES), 1)
    left = lane < SSD_HEAD_DIM

    def chunk_body(c, carry):
        r0 = pl.multiple_of(c * q, q)
        rows = pl.ds(r0, q)
        pr_c = pr_ref[rows, :]
        row_t = rowarg_ref[rows, :].T
        for g in range(SSD_N_GROUPS):
            gl = slice(g * SSD_GROUP_W, (g + 1) * SSD_GROUP_W)
            xs_g = xs_ref[rows, gl]
            b_g = xs_ref[rows, SSD_D_INNER + g * SSD_D_STATE:SSD_D_INNER + (g + 1) * SSD_D_STATE].astype(BF16)
            c_g = xs_ref[rows, SSD_D_INNER + SSD_BC_W + g * SSD_D_STATE:
                         SSD_D_INNER + SSD_BC_W + (g + 1) * SSD_D_STATE].astype(BF16)
            cb = _dot_nt(c_g, b_g)
            sf = sf_ref[g]
            y_off = (_dot(c_g, sf.astype(BF16)) * exf_ref[rows, gl]
                     + _dot(c_g, sbc_ref[c, g].astype(BF16)) * exb_ref[rows, gl])
            hpg = SSD_N_HEADS // SSD_N_GROUPS
            for pm in range(hpg // 2):
                ms = []
                for hh in range(2):
                    h = g * hpg + 2 * pm + hh
                    arg = jnp.where(
                        lower, pr_c[:, h:h + 1] + row_t[h:h + 1, :],
                        jnp.where(upper,
                                  pr_c[:, SSD_N_HEADS + h:SSD_N_HEADS + h + 1]
                                  + row_t[SSD_N_HEADS + h:SSD_N_HEADS + h + 1, :],
                                  row_t[2 * SSD_N_HEADS + h:2 * SSD_N_HEADS + h + 1, :]))
                    ms.append((cb * jnp.exp(arg)).astype(BF16))
                pl_ = slice(pm * LANES, (pm + 1) * LANES)
                x_pair = xs_g[:, pl_]
                x2 = jnp.concatenate([jnp.where(left, x_pair, 0.0), jnp.where(left, 0.0, x_pair)],
                                     axis=0).astype(BF16)
                y_pair = _dot(jnp.concatenate(ms, axis=1), x2)
                cl = slice(g * SSD_GROUP_W + pm * LANES, g * SSD_GROUP_W + (pm + 1) * LANES)
                y_ref[rows, cl] = y_pair + y_off[:, pl_] + drow_ref[:, cl] * x_pair
            xd = (xs_g * wf_ref[rows, gl]).astype(BF16)
            decf = exf_ref[pl.ds(r0 + q - 1, 1), gl]
            sf_ref[g] = decf * sf + _dot_tn(b_g, xd)
        return carry

    lax.fori_loop(0, nc, chunk_body, 0)

    z = z_ref[...].astype(F32)
    y = y_ref[...] * _silu(z)
    outs = []
    for g in range(SSD_N_GROUPS):
        gl = slice(g * SSD_GROUP_W, (g + 1) * SSD_GROUP_W)
        yg = y[:, gl]
        ms = jnp.mean(yg * yg, axis=-1, keepdims=True)
        outs.append((yg * lax.rsqrt(ms + EPS) * normw_ref[:, gl]).astype(BF16))
    yn = jnp.concatenate(outs, axis=1)
    o_ref[...] = _dot(yn, wout_ref[...]).astype(o_ref.dtype)


def _halo_specs(seq_len, tile, n_rows, width, col_block, tile_of):
    rb = tile // SSD_HALO
    seq_rb = seq_len // SSD_HALO
    last_rb = n_rows // SSD_HALO - 1

    def prev_map(b, t):
        return (jnp.maximum(b * seq_rb + tile_of(t) * rb - 1, 0), col_block)

    def cur_map(b, t):
        return (b * (seq_len // tile) + tile_of(t), col_block)

    def next_map(b, t):
        return (jnp.minimum(b * seq_rb + (tile_of(t) + 1) * rb, last_rb), col_block)

    return [pl.BlockSpec((SSD_HALO, width), prev_map),
            pl.BlockSpec((tile, width), cur_map),
            pl.BlockSpec((SSD_HALO, width), next_map)]


def _const_spec(arr):
    nd = arr.ndim
    return pl.BlockSpec(arr.shape, lambda b, t: (0,) * nd)


def _ssd(big, small, batch, seq_len, wts):
    nt = batch * seq_len
    tile = SSD_TILE
    n_tiles = seq_len // tile
    nc = tile // SSD_Q
    consts_pre = [wts["ssd_conv_w"], wts["ssd_conv_b"], wts["ssd_bias"], wts["ssd_nega"],
                  wts["ssd_tri"], wts["ssd_ones"], wts["ssd_eb2"]]
    rev = lambda t: n_tiles - 1 - t
    sb_in = pl.pallas_call(
        functools.partial(_ssd_pre_kernel, n_tiles=n_tiles),
        grid=(batch, n_tiles),
        in_specs=_halo_specs(seq_len, tile, nt, SSD_XBC, 0, rev)
        + [pl.BlockSpec((tile, SMALL_W), lambda b, t: (b * n_tiles + rev(t), 0))]
        + [_const_spec(c) for c in consts_pre],
        out_specs=pl.BlockSpec((1, 1, SSD_N_GROUPS, SSD_D_STATE, SSD_GROUP_W),
                               lambda b, t: (b, rev(t), 0, 0, 0)),
        out_shape=jax.ShapeDtypeStruct((batch, n_tiles, SSD_N_GROUPS, SSD_D_STATE, SSD_GROUP_W), F32),
        scratch_shapes=[pltpu.VMEM((tile + 2 * SSD_HALO, SSD_D_INNER + SSD_BC_W), F32),
                        pltpu.VMEM((tile, SSD_D_INNER + SSD_BC_W), F32),
                        pltpu.VMEM((tile, SSD_D_INNER), F32),
                        pltpu.VMEM((tile, SSD_D_INNER), F32),
                        pltpu.VMEM((SSD_N_GROUPS, SSD_D_STATE, SSD_GROUP_W), F32)],
        compiler_params=_cparams(2),
        name="ssd_pre",
    )(big, big, big, small, *consts_pre)

    consts = [wts["ssd_conv_w"], wts["ssd_conv_b"], wts["ssd_bias"], wts["ssd_nega"],
              wts["ssd_tri"], wts["ssd_ones"], wts["ssd_ef2"], wts["ssd_eb2"],
              wts["ssd_drow"], wts["ssd_normw"], wts["w_ssd_out"]]
    ident = lambda t: t
    z_block = (SSD_XBC + 2 * GLA_DK + 2 * GLA_DV) // SSD_D_INNER
    return pl.pallas_call(
        functools.partial(_ssd_main_kernel, n_tiles=n_tiles),
        grid=(batch, n_tiles),
        in_specs=_halo_specs(seq_len, tile, nt, SSD_XBC, 0, ident)
        + [pl.BlockSpec((tile, SSD_D_INNER), lambda b, t: (b * n_tiles + t, z_block)),
           pl.BlockSpec((tile, SMALL_W), lambda b, t: (b * n_tiles + t, 0)),
           pl.BlockSpec((1, 1, SSD_N_GROUPS, SSD_D_STATE, SSD_GROUP_W), lambda b, t: (b, t, 0, 0, 0))]
        + [_const_spec(c) for c in consts],
        out_specs=pl.BlockSpec((tile, D_MODEL), lambda b, t: (b * n_tiles + t, 0)),
        out_shape=jax.ShapeDtypeStruct((nt, D_MODEL), F32),
        scratch_shapes=[pltpu.VMEM((tile + 2 * SSD_HALO, SSD_XBC), F32),
                        pltpu.VMEM((tile, SSD_XBC), F32),
                        pltpu.VMEM((tile, SMALL_W), F32),
                        pltpu.VMEM((tile, SMALL_W), F32),
                        pltpu.VMEM((tile, SSD_D_INNER), F32),
                        pltpu.VMEM((tile, SSD_D_INNER), F32),
                        pltpu.VMEM((tile, SSD_D_INNER), F32),
                        pltpu.VMEM((tile, SSD_D_INNER), F32),
                        pltpu.VMEM((SSD_N_GROUPS, SSD_D_STATE, SSD_GROUP_W), F32),
                        pltpu.VMEM((nc, SSD_N_GROUPS, SSD_D_STATE, SSD_GROUP_W), F32),
                        pltpu.VMEM((tile, SSD_D_INNER), F32)],
        compiler_params=_cparams(2),
        name="ssd_main",
    )(big, big, big, big, small, sb_in, *consts)


def _gla_logdecay(small_ref, w2_ref, gb_ref, tri_ref, ones_ref):
    s = small_ref[...].astype(BF16)
    pre = _dot(s, w2_ref[...]) + gb_ref[...]
    la = -_softplus(-pre) * (1.0 / GLA_GATE_NORM)
    p = _dot_exact_lhs(tri_ref[...], la)
    tot = _dot_exact_lhs(ones_ref[...], la)
    col = lax.broadcasted_iota(jnp.int32, la.shape, 1)
    cs = jnp.where(col < GLA_DK, p, tot - p + la)
    return la, cs


def _gla_bwd_state_step(sb_h, k_h, v_h, la_h, cs_h, ones_col):
    dec = jnp.exp(_dot_exact_tn(la_h, ones_col))
    k_dec = (k_h * jnp.exp(cs_h[0:1, :] - cs_h)).astype(BF16)
    return dec * sb_h + _dot_tn(k_dec, v_h)


def _dot_exact_tn(x, ones_bf16, n=3):
    acc = None
    for p in _split_bf16(x, n):
        t = _dot_tn(p, ones_bf16)
        acc = t if acc is None else acc + t
    return acc


def _gla_pre_kernel(qkvg_ref, small_ref, w2_ref, gb_ref, tri_ref, ones_ref, onesc_ref, sbin_ref,
                    la_ref, cs_ref, sb_ref):
    t = pl.program_id(1)
    tile = GLA_TILE
    q = GLA_CHUNK
    nc = tile // q

    @pl.when(t == 0)
    def _():
        sb_ref[...] = jnp.zeros_like(sb_ref)

    sbin_ref[0, 0] = sb_ref[...]
    la, cs = _gla_logdecay(small_ref, w2_ref, gb_ref, tri_ref, ones_ref)
    la_ref[...] = la[:, GLA_DK:]
    cs_ref[...] = cs[:, GLA_DK:]

    def body(i, carry):
        c = nc - 1 - i
        rows = pl.ds(pl.multiple_of(c * q, q), q)
        for h in range(GLA_N_HEADS):
            kl = slice(h * GLA_HEAD_K, (h + 1) * GLA_HEAD_K)
            k_h = qkvg_ref[rows, GLA_DK + h * GLA_HEAD_K:GLA_DK + (h + 1) * GLA_HEAD_K].astype(F32)
            v_h = qkvg_ref[rows, 2 * GLA_DK + h * GLA_HEAD_V:2 * GLA_DK + (h + 1) * GLA_HEAD_V]
            sb_ref[h] = _gla_bwd_state_step(sb_ref[h], k_h, v_h, la_ref[rows, kl], cs_ref[rows, kl],
                                            onesc_ref[...])
        return carry

    lax.fori_loop(0, nc, body, 0)


def _gla_main_kernel(qkvg_ref, small_ref, sbin_ref, w2_ref, gb_ref, tri_ref, ones_ref, onesc_ref,
                     normw_ref, wout_ref, o_ref,
                     la_ref, cs_ref, sf_ref, sbc_ref, oacc_ref):
    t = pl.program_id(1)
    tile = GLA_TILE
    q = GLA_CHUNK
    nc = tile // q
    scale = GLA_HEAD_K ** -0.5

    @pl.when(t == 0)
    def _():
        sf_ref[...] = jnp.zeros_like(sf_ref)

    la, cs = _gla_logdecay(small_ref, w2_ref, gb_ref, tri_ref, ones_ref)
    la_ref[...] = la
    cs_ref[...] = cs

    sbc_ref[nc - 1] = sbin_ref[0, 0]

    def bwd_body(i, carry):
        c = nc - 1 - i
        rows = pl.ds(pl.multiple_of(c * q, q), q)
        for h in range(GLA_N_HEADS):
            kl = slice(GLA_DK + h * GLA_HEAD_K, GLA_DK + (h + 1) * GLA_HEAD_K)
            k_h = qkvg_ref[rows, kl].astype(F32)
            v_h = qkvg_ref[rows, 2 * GLA_DK + h * GLA_HEAD_V:2 * GLA_DK + (h + 1) * GLA_HEAD_V]
            sbc_ref[c - 1, h] = _gla_bwd_state_step(sbc_ref[c, h], k_h, v_h, la_ref[rows, kl],
                                                    cs_ref[rows, kl], onesc_ref[...])
        return carry

    lax.fori_loop(0, nc - 1, bwd_body, 0)

    ii = lax.broadcasted_iota(jnp.int32, (q, q), 0)
    jj = lax.broadcasted_iota(jnp.int32, (q, q), 1)
    lower_inc = ii >= jj
    upper_inc = ii <= jj

    def chunk_body(c, carry):
        rows = pl.ds(pl.multiple_of(c * q, q), q)
        for h in range(GLA_N_HEADS):
            fl = slice(h * GLA_HEAD_K, (h + 1) * GLA_HEAD_K)
            bl = slice(GLA_DK + h * GLA_HEAD_K, GLA_DK + (h + 1) * GLA_HEAD_K)
            q_h = qkvg_ref[rows, fl].astype(F32) * scale
            k_h = qkvg_ref[rows, bl].astype(F32)
            v_h = qkvg_ref[rows, 2 * GLA_DK + h * GLA_HEAD_V:2 * GLA_DK + (h + 1) * GLA_HEAD_V]
            csf = cs_ref[rows, fl]
            csb = cs_ref[rows, bl]
            qf = (q_h * jnp.exp(csf)).astype(BF16)
            kf = (k_h * jnp.exp(-csf)).astype(BF16)
            qb = (q_h * jnp.exp(csb)).astype(BF16)
            kb = (k_h * jnp.exp(-csb)).astype(BF16)
            att = (jnp.where(lower_inc, _dot_nt(qf, kf), 0.0)
                   + jnp.where(upper_inc, _dot_nt(qb, kb), 0.0))
            sf = sf_ref[h]
            s_cat = jnp.concatenate([sf, sbc_ref[c, h]], axis=0).astype(BF16)
            o = _dot(att.astype(BF16), v_h) + _dot(jnp.concatenate([qf, qb], axis=1), s_cat)
            oacc_ref[rows, h * GLA_HEAD_V:(h + 1) * GLA_HEAD_V] = o
            dec = jnp.exp(_dot_exact_tn(la_ref[rows, fl], onesc_ref[...]))
            k_dec = (k_h * jnp.exp(csf[q - 1:q, :] - csf)).astype(BF16)
            sf_ref[h] = dec * sf + _dot_tn(k_dec, v_h)
        return carry

    lax.fori_loop(0, nc, chunk_body, 0)

    g = qkvg_ref[:, 2 * GLA_DK + GLA_DV:2 * GLA_DK + 2 * GLA_DV].astype(F32)
    o = oacc_ref[...]
    outs = []
    for h in range(GLA_N_HEADS):
        hl = slice(h * GLA_HEAD_V, (h + 1) * GLA_HEAD_V)
        oh = o[:, hl]
        ms = jnp.mean(oh * oh, axis=-1, keepdims=True)
        outs.append((oh * lax.rsqrt(ms + EPS) * normw_ref[:, hl] * _silu(g[:, hl])).astype(BF16))
    o_ref[...] = _dot(jnp.concatenate(outs, axis=1), wout_ref[...]).astype(o_ref.dtype)


def _gla(big, small, batch, seq_len, wts):
    nt = batch * seq_len
    tile = GLA_TILE
    n_tiles = seq_len // tile
    nc = tile // GLA_CHUNK
    qkvg_w = 2 * GLA_DK + 2 * GLA_DV
    qkvg_block = SSD_XBC // qkvg_w
    rev = lambda t: n_tiles - 1 - t
    consts_pre = [wts["gla_w2"], wts["gla_gb"], wts["gla_tri"], wts["gla_ones"], wts["gla_onesc"]]
    sb_in = pl.pallas_call(
        _gla_pre_kernel,
        grid=(batch, n_tiles),
        in_specs=[pl.BlockSpec((tile, qkvg_w), lambda b, t: (b * n_tiles + rev(t), qkvg_block)),
                  pl.BlockSpec((tile, SMALL_W), lambda b, t: (b * n_tiles + rev(t), 0))]
        + [_const_spec(c) for c in consts_pre],
        out_specs=pl.BlockSpec((1, 1, GLA_N_HEADS, GLA_HEAD_K, GLA_HEAD_V), lambda b, t: (b, rev(t), 0, 0, 0)),
        out_shape=jax.ShapeDtypeStruct((batch, n_tiles, GLA_N_HEADS, GLA_HEAD_K, GLA_HEAD_V), F32),
        scratch_shapes=[pltpu.VMEM((tile, GLA_DK), F32),
                        pltpu.VMEM((tile, GLA_DK), F32),
                        pltpu.VMEM((GLA_N_HEADS, GLA_HEAD_K, GLA_HEAD_V), F32)],
        compiler_params=_cparams(2),
        name="gla_pre",
    )(big, small, *consts_pre)

    consts = consts_pre + [wts["gla_normw"], wts["w_gla_out"]]
    return pl.pallas_call(
        _gla_main_kernel,
        grid=(batch, n_tiles),
        in_specs=[pl.BlockSpec((tile, qkvg_w), lambda b, t: (b * n_tiles + t, qkvg_block)),
                  pl.BlockSpec((tile, SMALL_W), lambda b, t: (b * n_tiles + t, 0)),
                  pl.BlockSpec((1, 1, GLA_N_HEADS, GLA_HEAD_K, GLA_HEAD_V), lambda b, t: (b, t, 0, 0, 0))]
        + [_const_spec(c) for c in consts],
        out_specs=pl.BlockSpec((tile, D_MODEL), lambda b, t: (b * n_tiles + t, 0)),
        out_shape=jax.ShapeDtypeStruct((nt, D_MODEL), F32),
        scratch_shapes=[pltpu.VMEM((tile, 2 * GLA_DK), F32),
                        pltpu.VMEM((tile, 2 * GLA_DK), F32),
                        pltpu.VMEM((GLA_N_HEADS, GLA_HEAD_K, GLA_HEAD_V), F32),
                        pltpu.VMEM((nc, GLA_N_HEADS, GLA_HEAD_K, GLA_HEAD_V), F32),
                        pltpu.VMEM((tile, GLA_DV), F32)],
        compiler_params=_cparams(2),
        name="gla_main",
    )(big, small, sb_in, *consts)


def _merge_kernel(x_ref, a_ref, b_ref, ga_ref, gb_ref, wo_ref, o_ref):
    mixed = (_sigmoid(ga_ref[...].astype(F32)) * a_ref[...]
             + _sigmoid(gb_ref[...].astype(F32)) * b_ref[...])
    o_ref[...] = x_ref[...] + _dot(mixed.astype(BF16), wo_ref[...])


def _merge(x, br_a, br_b, big, w_o):
    nt = x.shape[0]
    tm = min(PROJ_TM, nt)
    gate_block = (BIG_W - 2 * D_MODEL) // D_MODEL
    row = lambda i: (i, 0)
    return pl.pallas_call(
        _merge_kernel,
        grid=(nt // tm,),
        in_specs=[pl.BlockSpec((tm, D_MODEL), row),
                  pl.BlockSpec((tm, D_MODEL), row),
                  pl.BlockSpec((tm, D_MODEL), row),
                  pl.BlockSpec((tm, D_MODEL), lambda i: (i, gate_block)),
                  pl.BlockSpec((tm, D_MODEL), lambda i: (i, gate_block + 1)),
                  pl.BlockSpec((D_MODEL, D_MODEL), lambda i: (0, 0))],
        out_specs=pl.BlockSpec((tm, D_MODEL), row),
        out_shape=jax.ShapeDtypeStruct((nt, D_MODEL), F32),
        compiler_params=_cparams(1),
        name="merge",
    )(x, br_a, br_b, big, big, w_o)


def _ffn_kernel(prev_ref, cur_ref, next_ref, n2_ref, wg_ref, wv_ref, cwg_ref, cwv_ref, cbg_ref, cbv_ref,
                wd_ref, fn_ref, o_ref, h_ref, ug_ref, uv_ref, acc_ref, *, n_tiles):
    t = pl.program_id(1)
    f = pl.program_id(2)
    nf = pl.num_programs(2)
    tile = FFN_TILE

    def norm_rows(x):
        ms = jnp.mean(x * x, axis=-1, keepdims=True)
        return x * lax.rsqrt(ms + EPS) * n2_ref[...]

    @pl.when(f == 0)
    def _():
        h_ref[0:HALO, :] = norm_rows(prev_ref[...])
        h_ref[HALO:HALO + tile, :] = norm_rows(cur_ref[...])
        h_ref[HALO + tile:HALO + tile + HALO, :] = norm_rows(next_ref[...])
        acc_ref[...] = jnp.zeros_like(acc_ref)

    h = h_ref[...].astype(BF16)
    row = lax.broadcasted_iota(jnp.int32, (tile + 2 * HALO, 1), 0)
    keep = jnp.logical_and(jnp.logical_or(row >= HALO, t > 0),
                           jnp.logical_or(row < HALO + tile, t < n_tiles - 1))
    ug_ref[...] = jnp.where(keep, _dot(h, wg_ref[...]), 0.0)
    uv_ref[...] = jnp.where(keep, _dot(h, wv_ref[...]), 0.0)

    def conv3(u_ref, cw_ref, cb_ref):
        acc = None
        for k in range(3):
            term = u_ref[HALO - 1 + k:HALO - 1 + k + tile, :] * cw_ref[k:k + 1, :]
            acc = term if acc is None else acc + term
        return acc + cb_ref[...]

    act = _silu(conv3(ug_ref, cwg_ref, cbg_ref)) * conv3(uv_ref, cwv_ref, cbv_ref)
    acc_ref[...] += _dot(act.astype(BF16), wd_ref[...])

    @pl.when(f == nf - 1)
    def _():
        x2 = cur_ref[...] + acc_ref[...]
        ms = jnp.mean(x2 * x2, axis=-1, keepdims=True)
        o_ref[...] = x2 * lax.rsqrt(ms + EPS) * fn_ref[...]


def _ffn(x1, batch, seq_len, wts):
    nt = batch * seq_len
    tile = FFN_TILE
    n_tiles = seq_len // tile
    nf = D_FF // FFN_F
    rb = tile // HALO
    seq_rb = seq_len // HALO
    last_rb = nt // HALO - 1
    c2 = lambda b, t, f: (0, 0)
    return pl.pallas_call(
        functools.partial(_ffn_kernel, n_tiles=n_tiles),
        grid=(batch, n_tiles, nf),
        in_specs=[pl.BlockSpec((HALO, D_MODEL), lambda b, t, f: (jnp.maximum(b * seq_rb + t * rb - 1, 0), 0)),
                  pl.BlockSpec((tile, D_MODEL), lambda b, t, f: (b * n_tiles + t, 0)),
                  pl.BlockSpec((HALO, D_MODEL),
                               lambda b, t, f: (jnp.minimum(b * seq_rb + (t + 1) * rb, last_rb), 0)),
                  pl.BlockSpec((1, D_MODEL), c2),
                  pl.BlockSpec((D_MODEL, FFN_F), lambda b, t, f: (0, f)),
                  pl.BlockSpec((D_MODEL, FFN_F), lambda b, t, f: (0, nf + f)),
                  pl.BlockSpec((3, FFN_F), lambda b, t, f: (0, f)),
                  pl.BlockSpec((3, FFN_F), lambda b, t, f: (0, nf + f)),
                  pl.BlockSpec((1, FFN_F), lambda b, t, f: (0, f)),
                  pl.BlockSpec((1, FFN_F), lambda b, t, f: (0, nf + f)),
                  pl.BlockSpec((FFN_F, D_MODEL), lambda b, t, f: (f, 0)),
                  pl.BlockSpec((1, D_MODEL), c2)],
        out_specs=pl.BlockSpec((tile, D_MODEL), lambda b, t, f: (b * n_tiles + t, 0)),
        out_shape=jax.ShapeDtypeStruct((nt, D_MODEL), F32),
        scratch_shapes=[pltpu.VMEM((tile + 2 * HALO, D_MODEL), F32),
                        pltpu.VMEM((tile + 2 * HALO, FFN_F), F32),
                        pltpu.VMEM((tile + 2 * HALO, FFN_F), F32),
                        pltpu.VMEM((tile, D_MODEL), F32)],
        compiler_params=_cparams(3),
        name="ffn",
    )(x1, x1, x1, wts["norm2_w"], wts["w_ffn_up"], wts["w_ffn_up"], wts["ffn_conv_w"], wts["ffn_conv_w"],
      wts["ffn_conv_b"], wts["ffn_conv_b"], wts["w_ffn_down"], wts["final_norm_w"])


def _block_tri(n, blk, inclusive_lower):
    i = np.arange(n)[:, None]
    j = np.arange(n)[None, :]
    same = (i // blk) == (j // blk)
    m = same & (i >= j) if inclusive_lower else same
    return jnp.asarray(m, dtype=BF16)


def _prep_weights(norm1_w, w_in, ssd_conv_w, ssd_conv_b, ssd_a_log, ssd_dt_bias, ssd_d, ssd_norm_w, w_ssd_out,
                  gla_gate_w2, gla_gate_b, gla_norm_w, w_gla_out, w_o, norm2_w, w_ffn_up, ffn_conv_w, ffn_conv_b,
                  w_ffn_down, final_norm_w):
    o_z = 0
    o_xbc = o_z + SSD_D_INNER
    o_dt = o_xbc + SSD_XBC
    o_q = o_dt + 2 * SSD_N_HEADS
    o_k = o_q + GLA_DK
    o_v = o_k + GLA_DK
    o_g = o_v + GLA_DV
    o_gk = o_g + GLA_DV
    o_gate = o_gk + 2 * GLA_GATE_RANK
    w_big = jnp.concatenate([w_in[:, o_xbc:o_dt], w_in[:, o_q:o_gk], w_in[:, o_z:o_xbc], w_in[:, o_gate:]],
                            axis=1).astype(BF16)
    pad = SMALL_W - 2 * SSD_N_HEADS - 2 * GLA_GATE_RANK
    w_small = jnp.concatenate([w_in[:, o_dt:o_q], w_in[:, o_gk:o_gate], jnp.zeros((D_MODEL, pad), F32)],
                              axis=1).astype(BF16)

    def row128(v):
        return jnp.concatenate([v.reshape(-1), jnp.zeros((SMALL_W - v.size,), F32)]).reshape(1, SMALL_W)

    head_of_lane = np.arange(SSD_D_INNER) // SSD_HEAD_DIM
    ef = (np.arange(SMALL_W)[:, None] == head_of_lane[None, :])
    eb = (np.arange(SMALL_W)[:, None] == (head_of_lane[None, :] + SSD_N_HEADS))
    w2 = jnp.zeros((SMALL_W, 2 * GLA_DK), F32)
    w2 = w2.at[GK_OFF:GK_OFF + GLA_GATE_RANK, :GLA_DK].set(gla_gate_w2[0])
    w2 = w2.at[GK_OFF + GLA_GATE_RANK:GK_OFF + 2 * GLA_GATE_RANK, GLA_DK:].set(gla_gate_w2[1])
    return {
        "norm1_w": norm1_w.reshape(1, D_MODEL),
        "w_big": w_big,
        "w_small": w_small,
        "ssd_conv_w": ssd_conv_w,
        "ssd_conv_b": ssd_conv_b.reshape(1, SSD_XBC),
        "ssd_bias": row128(ssd_dt_bias),
        "ssd_nega": row128(-jnp.exp(ssd_a_log)),
        "ssd_tri": _block_tri(SSD_TILE, SSD_Q, True),
        "ssd_ones": _block_tri(SSD_TILE, SSD_Q, False),
        "ssd_ef2": jnp.asarray(np.concatenate([ef, ef], axis=0), dtype=BF16),
        "ssd_eb2": jnp.asarray(np.concatenate([eb, eb], axis=0), dtype=BF16),
        "ssd_drow": jnp.repeat(ssd_d, SSD_HEAD_DIM).reshape(1, SSD_D_INNER),
        "ssd_normw": ssd_norm_w.reshape(1, SSD_D_INNER),
        "w_ssd_out": w_ssd_out.astype(BF16),
        "gla_w2": w2.astype(BF16),
        "gla_gb": gla_gate_b.reshape(1, 2 * GLA_DK),
        "gla_tri": _block_tri(GLA_TILE, GLA_CHUNK, True),
        "gla_ones": _block_tri(GLA_TILE, GLA_CHUNK, False),
        "gla_onesc": jnp.ones((GLA_CHUNK, GLA_HEAD_V), BF16),
        "gla_normw": gla_norm_w.reshape(1, GLA_DV),
        "w_gla_out": w_gla_out.astype(BF16),
        "w_o": w_o.astype(BF16),
        "norm2_w": norm2_w.reshape(1, D_MODEL),
        "w_ffn_up": w_ffn_up.astype(BF16),
        "ffn_conv_w": ffn_conv_w,
        "ffn_conv_b": ffn_conv_b.reshape(1, 2 * D_FF),
        "w_ffn_down": w_ffn_down.astype(BF16),
        "final_norm_w": final_norm_w.reshape(1, D_MODEL),
    }


def _trunk(x3, wts):
    batch, seq_len, _ = x3.shape
    x = x3.reshape(batch * seq_len, D_MODEL)
    big = _norm_proj(x, wts["norm1_w"], wts["w_big"], BF16, PROJ_TN)
    small = _norm_proj(x, wts["norm1_w"], wts["w_small"], F32, SMALL_W)
    br_a = _ssd(big, small, batch, seq_len, wts)
    br_b = _gla(big, small, batch, seq_len, wts)
    x1 = _merge(x, br_a, br_b, big, wts["w_o"])
    out = _ffn(x1, batch, seq_len, wts)
    return out.reshape(batch, seq_len, D_MODEL)


def kernel(x_prompt, x_sample, norm1_w, w_in, ssd_conv_w, ssd_conv_b, ssd_a_log, ssd_dt_bias, ssd_d, ssd_norm_w,
           w_ssd_out, gla_gate_w2, gla_gate_b, gla_norm_w, w_gla_out, w_o, norm2_w, w_ffn_up, ffn_conv_w,
           ffn_conv_b, w_ffn_down, final_norm_w):
    wts = _prep_weights(norm1_w[0], w_in[0], ssd_conv_w[0], ssd_conv_b[0], ssd_a_log[0], ssd_dt_bias[0], ssd_d[0],
                        ssd_norm_w[0], w_ssd_out[0], gla_gate_w2[0], gla_gate_b[0], gla_norm_w[0], w_gla_out[0],
                        w_o[0], norm2_w[0], w_ffn_up[0], ffn_conv_w[0], ffn_conv_b[0], w_ffn_down[0], final_norm_w)
    return (_trunk(x_prompt, wts), _trunk(x_sample, wts))
```

```python
import functools

import numpy as np
import jax
import jax.numpy as jnp
from jax import lax
from jax.experimental import pallas as pl
from jax.experimental.pallas import tpu as pltpu

F32 = jnp.float32
BF16 = jnp.bfloat16

D_MODEL = 1024
SSD_D_INNER = 2048
SSD_HEAD_DIM = 64
SSD_N_HEADS = 32
SSD_N_GROUPS = 4
SSD_D_STATE = 128
SSD_GROUP_W = SSD_D_INNER // SSD_N_GROUPS
SSD_BC_W = SSD_N_GROUPS * SSD_D_STATE
SSD_XBC = SSD_D_INNER + 2 * SSD_BC_W
SSD_CONV = 5
GLA_N_HEADS = 4
GLA_DK = 512
GLA_DV = 1024
GLA_HEAD_K = 128
GLA_HEAD_V = 256
GLA_GATE_RANK = 16
GLA_GATE_NORM = 16.0
GLA_CHUNK = 64
D_FF = 2816
EPS = 1e-6

LANES = 128
SUBLANES = 8
VMEM_LIMIT = 56 * 1024 * 1024

BIG_W = SSD_XBC + 2 * GLA_DK + 2 * GLA_DV + SSD_D_INNER + 2 * D_MODEL
SMALL_W = LANES
GK_OFF = 2 * SSD_N_HEADS

SSD_TILE = 256
SSD_Q = 64
GLA_TILE = 256
HALO = SUBLANES
SSD_HALO = 2 * SUBLANES
CONV_SLAB = 512
FFN_TILE = 512
FFN_F = 1408
PROJ_TM = 1024
PROJ_TN = 1024


def _dot(a, b):
    return jnp.dot(a, b, preferred_element_type=F32)


def _dot_tn(a, b):
    return lax.dot_general(a, b, (((0,), (0,)), ((), ())), preferred_element_type=F32)


def _dot_nt(a, b):
    return lax.dot_general(a, b, (((1,), (1,)), ((), ())), preferred_element_type=F32)


def _split_bf16(x, n):
    parts = []
    r = x
    for i in range(n):
        p = r.astype(BF16)
        parts.append(p)
        if i + 1 < n:
            r = r - p.astype(F32)
    return parts


def _dot_exact_lhs(m_bf16, x, n=3):
    acc = None
    for p in _split_bf16(x, n):
        t = _dot(m_bf16, p)
        acc = t if acc is None else acc + t
    return acc


def _sigmoid(x):
    return 0.5 * jnp.tanh(0.5 * x) + 0.5


def _silu(x):
    return x * _sigmoid(x)


def _softplus(x):
    return jnp.maximum(x, 0.0) + jnp.log1p(jnp.exp(-jnp.abs(x)))


def _cparams(n_axes):
    return pltpu.CompilerParams(dimension_semantics=("arbitrary",) * n_axes,
                                vmem_limit_bytes=VMEM_LIMIT)


def _proj_kernel(x_ref, nw_ref, w_ref, o_ref, h_ref):
    @pl.when(pl.program_id(1) == 0)
    def _():
        x = x_ref[...]
        ms = jnp.mean(x * x, axis=-1, keepdims=True)
        h_ref[...] = (x * lax.rsqrt(ms + EPS) * nw_ref[...]).astype(BF16)

    o_ref[...] = _dot(h_ref[...], w_ref[...]).astype(o_ref.dtype)


def _norm_proj(x, nw, w, out_dtype, tn):
    nt, d = x.shape
    n = w.shape[1]
    tm = min(PROJ_TM, nt)
    return pl.pallas_call(
        _proj_kernel,
        grid=(nt // tm, n // tn),
        in_specs=[pl.BlockSpec((tm, d), lambda i, j: (i, 0)),
                  pl.BlockSpec((1, d), lambda i, j: (0, 0)),
                  pl.BlockSpec((d, tn), lambda i, j: (0, j))],
        out_specs=pl.BlockSpec((tm, tn), lambda i, j: (i, j)),
        out_shape=jax.ShapeDtypeStruct((nt, n), out_dtype),
        scratch_shapes=[pltpu.VMEM((tm, d), BF16)],
        compiler_params=_cparams(2),
        name="norm_proj",
    )(x, nw, w)


def _fill_ext(ext_ref, prev_ref, cur_ref, next_ref, first, last, tile):
    halo = SSD_HALO
    ext_ref[0:halo, :] = jnp.where(first, 0.0, prev_ref[...].astype(F32))
    ext_ref[halo:halo + tile, :] = cur_ref[...].astype(F32)
    ext_ref[halo + tile:halo + tile + halo, :] = jnp.where(last, 0.0, next_ref[...].astype(F32))


def _conv_taps(ext_ref, w_ref, b_ref, tile, lo, hi, n_taps):
    pad = (n_taps - 1) // 2
    acc = None
    for k in range(n_taps):
        xk = ext_ref[SSD_HALO - pad + k:SSD_HALO - pad + k + tile, lo:hi]
        t = xk * w_ref[k:k + 1, lo:hi]
        acc = t if acc is None else acc + t
    return acc + b_ref[:, lo:hi]


def _ssd_small(small_ref, bias_ref, nega_ref, tri_ref, ones_ref):
    s = small_ref[...]
    dt = _softplus(s + bias_ref[...])
    a = dt * nega_ref[...]
    p = _dot_exact_lhs(tri_ref[...], a)
    tot = _dot_exact_lhs(ones_ref[...], a)
    col = lax.broadcasted_iota(jnp.int32, s.shape, 1)
    fwd = col < SSD_N_HEADS
    r = tot - p + a
    pr = jnp.where(fwd, p, r)
    w = dt * jnp.exp(jnp.where(fwd, tot - p, p - a))
    rowarg = jnp.log(dt) - pr
    dsum = pltpu.roll(dt, 2 * SSD_N_HEADS, 1) + pltpu.roll(dt, SSD_N_HEADS, 1)
    diag_cols = (col >= 2 * SSD_N_HEADS) & (col < 3 * SSD_N_HEADS)
    rowarg = jnp.where(diag_cols, jnp.log(dsum), rowarg)
    return pr, w, rowarg


def _expand2(x, e2_ref):
    hi, lo = _split_bf16(x, 2)
    return _dot(jnp.concatenate([hi, lo], axis=1), e2_ref[...])


def _ssd_bwd_state_step(sb, xs_c, b_c, wb_c, decb_row):
    out = []
    for g in range(SSD_N_GROUPS):
        gl = slice(g * SSD_GROUP_W, (g + 1) * SSD_GROUP_W)
        xd = (xs_c[:, gl].astype(F32) * wb_c[:, gl]).astype(BF16)
        bg = b_c[:, g * SSD_D_STATE:(g + 1) * SSD_D_STATE].astype(BF16)
        out.append(decb_row[:, gl] * sb[g] + _dot_tn(bg, xd))
    return out


def _ssd_pre_kernel(prev_ref, cur_ref, next_ref, small_ref, convw_ref, convb_ref, bias_ref, nega_ref,
                    tri_ref, ones_ref, eb2_ref, sbin_ref, xconv_ref,
                    ext_ref, xs_ref, sb_ref, *, n_tiles):
    t = pl.program_id(1)
    tile = SSD_TILE

    @pl.when(t == 0)
    def _():
        sb_ref[...] = jnp.zeros_like(sb_ref)

    sbin_ref[0, 0] = sb_ref[...]

    _fill_ext(ext_ref, prev_ref, cur_ref, next_ref, t == n_tiles - 1, t == 0, tile)
    for lo in range(0, SSD_XBC, CONV_SLAB):
        xs = _silu(_conv_taps(ext_ref, convw_ref, convb_ref, tile, lo, lo + CONV_SLAB, SSD_CONV))
        xs_ref[:, lo:lo + CONV_SLAB] = xs
        xconv_ref[:, lo:lo + CONV_SLAB] = xs.astype(BF16)

    pr, w, _ = _ssd_small(small_ref, bias_ref, nega_ref, tri_ref, ones_ref)
    wb = _dot(w.astype(BF16), eb2_ref[0:LANES, :])
    decb = _expand2(jnp.exp(pr[0:SUBLANES, :]), eb2_ref)[0:1, :]
    sb = [sb_ref[g] for g in range(SSD_N_GROUPS)]
    new = _ssd_bwd_state_step(sb, xs_ref[:, 0:SSD_D_INNER], xs_ref[:, SSD_D_INNER:SSD_D_INNER + SSD_BC_W],
                              wb, decb)
    for g in range(SSD_N_GROUPS):
        sb_ref[g] = new[g]


def _ssd_main_kernel(xc_ref, z_ref, small_ref, sbin_ref,
                     bias_ref, nega_ref, tri_ref, ones_ref, ef2_ref, eb2_ref,
                     drow_ref, normw_ref, wout_ref, o_ref,
                     rowarg_ref, prf_ref, prb_ref, wf_ref, wb_ref, exf_ref, exb_ref,
                     sf_ref, sbc_ref, y_ref):
    t = pl.program_id(1)
    tile = SSD_TILE
    q = SSD_Q
    nc = tile // q
    b_off = SSD_D_INNER
    c_off = SSD_D_INNER + SSD_BC_W

    @pl.when(t == 0)
    def _():
        sf_ref[...] = jnp.zeros_like(sf_ref)

    pr, w, rowarg = _ssd_small(small_ref, bias_ref, nega_ref, tri_ref, ones_ref)
    rowarg_ref[...] = rowarg
    wbf = w.astype(BF16)
    wf_ref[...] = _dot(wbf, ef2_ref[0:LANES, :])
    wb_ref[...] = _dot(wbf, eb2_ref[0:LANES, :])
    prf_ref[...] = _expand2(pr, ef2_ref)
    prb_ref[...] = _expand2(pr, eb2_ref)
    ex = jnp.exp(pr)
    exf_ref[...] = _expand2(ex, ef2_ref)
    exb_ref[...] = _expand2(ex, eb2_ref)

    sbc_ref[nc - 1] = sbin_ref[0, 0]

    def bwd_body(k, carry):
        c = nc - 1 - k
        r0 = pl.multiple_of(c * q, q)
        rows = pl.ds(r0, q)
        sb = [sbc_ref[c, g] for g in range(SSD_N_GROUPS)]
        new = _ssd_bwd_state_step(sb, xc_ref[rows, 0:SSD_D_INNER], xc_ref[rows, b_off:c_off],
                                  wb_ref[rows, :], exb_ref[pl.ds(r0, 1), :])
        for g in range(SSD_N_GROUPS):
            sbc_ref[c - 1, g] = new[g]
        return carry

    lax.fori_loop(0, nc - 1, bwd_body, 0)

    ii = lax.broadcasted_iota(jnp.int32, (q, LANES), 0)
    ll = lax.broadcasted_iota(jnp.int32, (q, LANES), 1)
    jj = jnp.where(ll < SSD_HEAD_DIM, ll, ll - SSD_HEAD_DIM)
    lower = ii > jj
    upper = ii < jj
    left = ll < SSD_HEAD_DIM
    left_row = lax.broadcasted_iota(jnp.int32, (1, LANES), 1) < SSD_HEAD_DIM
    hpg = SSD_N_HEADS // SSD_N_GROUPS

    def pair_body(k, carry):
        k0 = pl.multiple_of(k * 2 * q, 2 * q)
        row_t = rowarg_ref[pl.ds(k0, 2 * q), :].T
        row_tr = pltpu.roll(row_t, q, 1)
        for s in range(2):
            r0 = pl.multiple_of(k0 + s * q, q)
            rows = pl.ds(r0, q)
            c = 2 * k + s
            lo_src, hi_src = (row_t, row_tr) if s == 0 else (row_tr, row_t)

            def pair_row(base, h0):
                return jnp.where(left_row, lo_src[base + h0:base + h0 + 1, :],
                                 hi_src[base + h0 + 1:base + h0 + 2, :])

            for g in range(SSD_N_GROUPS):
                gl = slice(g * SSD_GROUP_W, (g + 1) * SSD_GROUP_W)
                b_g = xc_ref[rows, b_off + g * SSD_D_STATE:b_off + (g + 1) * SSD_D_STATE]
                c_g = xc_ref[rows, c_off + g * SSD_D_STATE:c_off + (g + 1) * SSD_D_STATE]
                cb2 = _dot_nt(c_g, jnp.concatenate([b_g, b_g], axis=0))
                sf = sf_ref[g]
                y_off = (_dot(c_g, sf.astype(BF16)) * exf_ref[rows, gl]
                         + _dot(c_g, sbc_ref[c, g].astype(BF16)) * exb_ref[rows, gl])
                xs_g = xc_ref[rows, gl].astype(F32)
                for pm in range(hpg // 2):
                    h0 = g * hpg + 2 * pm
                    cl = slice(g * SSD_GROUP_W + pm * LANES, g * SSD_GROUP_W + (pm + 1) * LANES)
                    arg = jnp.where(
                        lower, prf_ref[rows, cl] + pair_row(0, h0),
                        jnp.where(upper, prb_ref[rows, cl] + pair_row(SSD_N_HEADS, h0),
                                  pair_row(2 * SSD_N_HEADS, h0)))
                    m2 = (cb2 * jnp.exp(arg)).astype(BF16)
                    x_pair = xs_g[:, pm * LANES:(pm + 1) * LANES]
                    x2 = jnp.concatenate([jnp.where(left, x_pair, 0.0), jnp.where(left, 0.0, x_pair)],
                                         axis=0).astype(BF16)
                    y_ref[rows, cl] = (_dot(m2, x2) + y_off[:, pm * LANES:(pm + 1) * LANES]
                                       + drow_ref[:, cl] * x_pair)
                xd = (xs_g * wf_ref[rows, gl]).astype(BF16)
                decf = exf_ref[pl.ds(r0 + q - 1, 1), gl]
                sf_ref[g] = decf * sf + _dot_tn(b_g, xd)
        return carry

    lax.fori_loop(0, nc // 2, pair_body, 0)

    z = z_ref[...].astype(F32)
    y = y_ref[...] * _silu(z)
    outs = []
    for g in range(SSD_N_GROUPS):
        gl = slice(g * SSD_GROUP_W, (g + 1) * SSD_GROUP_W)
        yg = y[:, gl]
        ms = jnp.mean(yg * yg, axis=-1, keepdims=True)
        outs.append((yg * lax.rsqrt(ms + EPS) * normw_ref[:, gl]).astype(BF16))
    yn = jnp.concatenate(outs, axis=1)
    o_ref[...] = _dot(yn, wout_ref[...]).astype(o_ref.dtype)


def _const_spec(arr):
    nd = arr.ndim
    return pl.BlockSpec(arr.shape, lambda b, t: (0,) * nd)


def _ssd(big, small, batch, seq_len, wts):
    nt = batch * seq_len
    tile = SSD_TILE
    n_tiles = seq_len // tile
    nc = tile // SSD_Q
    rb = tile // SSD_HALO
    seq_rb = seq_len // SSD_HALO
    last_rb = nt // SSD_HALO - 1
    state_shape = (SSD_N_GROUPS, SSD_D_STATE, SSD_GROUP_W)
    rev = lambda t: n_tiles - 1 - t
    consts_pre = [wts["ssd_conv_w"], wts["ssd_conv_b"], wts["ssd_bias"], wts["ssd_nega"],
                  wts["ssd_tri_full"], wts["ssd_ones_full"], wts["ssd_eb2"]]
    sb_in, xconv = pl.pallas_call(
        functools.partial(_ssd_pre_kernel, n_tiles=n_tiles),
        grid=(batch, n_tiles),
        in_specs=[pl.BlockSpec((SSD_HALO, SSD_XBC),
                               lambda b, t: (jnp.maximum(b * seq_rb + rev(t) * rb - 1, 0), 0)),
                  pl.BlockSpec((tile, SSD_XBC), lambda b, t: (b * n_tiles + rev(t), 0)),
                  pl.BlockSpec((SSD_HALO, SSD_XBC),
                               lambda b, t: (jnp.minimum(b * seq_rb + (rev(t) + 1) * rb, last_rb), 0)),
                  pl.BlockSpec((tile, SMALL_W), lambda b, t: (b * n_tiles + rev(t), 0))]
        + [_const_spec(c) for c in consts_pre],
        out_specs=[pl.BlockSpec((1, 1) + state_shape, lambda b, t: (b, rev(t), 0, 0, 0)),
                   pl.BlockSpec((tile, SSD_XBC), lambda b, t: (b * n_tiles + rev(t), 0))],
        out_shape=[jax.ShapeDtypeStruct((batch, n_tiles) + state_shape, F32),
                   jax.ShapeDtypeStruct((nt, SSD_XBC), BF16)],
        scratch_shapes=[pltpu.VMEM((tile + 2 * SSD_HALO, SSD_XBC), F32),
                        pltpu.VMEM((tile, SSD_XBC), F32),
                        pltpu.VMEM(state_shape, F32)],
        compiler_params=_cparams(2),
        name="ssd_pre",
    )(big, big, big, small, *consts_pre)

    consts = [wts["ssd_bias"], wts["ssd_nega"], wts["ssd_tri"], wts["ssd_ones"], wts["ssd_ef2"], wts["ssd_eb2"],
              wts["ssd_drow"], wts["ssd_normw"], wts["w_ssd_out"]]
    z_block = (SSD_XBC + 2 * GLA_DK + 2 * GLA_DV) // SSD_D_INNER
    wide = pltpu.VMEM((tile, SSD_D_INNER), F32)
    return pl.pallas_call(
        _ssd_main_kernel,
        grid=(batch, n_tiles),
        in_specs=[pl.BlockSpec((tile, SSD_XBC), lambda b, t: (b * n_tiles + t, 0)),
                  pl.BlockSpec((tile, SSD_D_INNER), lambda b, t: (b * n_tiles + t, z_block)),
                  pl.BlockSpec((tile, SMALL_W), lambda b, t: (b * n_tiles + t, 0)),
                  pl.BlockSpec((1, 1) + state_shape, lambda b, t: (b, t, 0, 0, 0))]
        + [_const_spec(c) for c in consts],
        out_specs=pl.BlockSpec((tile, D_MODEL), lambda b, t: (b * n_tiles + t, 0)),
        out_shape=jax.ShapeDtypeStruct((nt, D_MODEL), F32),
        scratch_shapes=[pltpu.VMEM((tile, SMALL_W), F32),
                        wide, wide, wide, wide, wide, wide,
                        pltpu.VMEM(state_shape, F32),
                        pltpu.VMEM((nc,) + state_shape, F32),
                        wide],
        compiler_params=_cparams(2),
        name="ssd_main",
    )(xconv, big, small, sb_in, *consts)


def _gla_logdecay(small_ref, w2_ref, gb_ref, tri_ref, ones_ref):
    s = small_ref[...].astype(BF16)
    pre = _dot(s, w2_ref[...]) + gb_ref[...]
    la = -_softplus(-pre) * (1.0 / GLA_GATE_NORM)
    p = _dot_exact_lhs(tri_ref[...], la)
    tot = _dot_exact_lhs(ones_ref[...], la)
    col = lax.broadcasted_iota(jnp.int32, la.shape, 1)
    return jnp.where(col < GLA_DK, p, tot - p + la)


def _gla_bwd_state_step(sb_h, k_h, v_h, cs_h):
    tot = cs_h[0:1, :]
    k_dec = (k_h * jnp.exp(tot - cs_h)).astype(BF16)
    return jnp.exp(tot) * sb_h + _dot_tn(v_h, k_dec)


def _gla_pre_kernel(qkvg_ref, small_ref, w2_ref, gb_ref, tri_ref, ones_ref, sbin_ref,
                    cs_ref, sb_ref):
    t = pl.program_id(1)
    tile = GLA_TILE
    q = GLA_CHUNK
    nc = tile // q

    @pl.when(t == 0)
    def _():
        sb_ref[...] = jnp.zeros_like(sb_ref)

    sbin_ref[0, 0] = sb_ref[...]
    cs = _gla_logdecay(small_ref, w2_ref, gb_ref, tri_ref, ones_ref)
    cs_ref[...] = cs[:, GLA_DK:]

    def body(i, carry):
        c = nc - 1 - i
        rows = pl.ds(pl.multiple_of(c * q, q), q)
        for h in range(GLA_N_HEADS):
            kl = slice(h * GLA_HEAD_K, (h + 1) * GLA_HEAD_K)
            k_h = qkvg_ref[rows, GLA_DK + h * GLA_HEAD_K:GLA_DK + (h + 1) * GLA_HEAD_K].astype(F32)
            v_h = qkvg_ref[rows, 2 * GLA_DK + h * GLA_HEAD_V:2 * GLA_DK + (h + 1) * GLA_HEAD_V]
            sb_ref[h] = _gla_bwd_state_step(sb_ref[h], k_h, v_h, cs_ref[rows, kl])
        return carry

    lax.fori_loop(0, nc, body, 0)


def _gla_main_kernel(qkvg_ref, small_ref, sbin_ref, w2_ref, gb_ref, tri_ref, ones_ref,
                     normw_ref, wout_ref, o_ref,
                     cs_ref, sf_ref, sbc_ref, oacc_ref):
    t = pl.program_id(1)
    tile = GLA_TILE
    q = GLA_CHUNK
    nc = tile // q
    scale = GLA_HEAD_K ** -0.5

    @pl.when(t == 0)
    def _():
        sf_ref[...] = jnp.zeros_like(sf_ref)

    cs_ref[...] = _gla_logdecay(small_ref, w2_ref, gb_ref, tri_ref, ones_ref)

    sbc_ref[nc - 1] = sbin_ref[0, 0]

    def bwd_body(i, carry):
        c = nc - 1 - i
        rows = pl.ds(pl.multiple_of(c * q, q), q)
        for h in range(GLA_N_HEADS):
            kl = slice(GLA_DK + h * GLA_HEAD_K, GLA_DK + (h + 1) * GLA_HEAD_K)
            k_h = qkvg_ref[rows, kl].astype(F32)
            v_h = qkvg_ref[rows, 2 * GLA_DK + h * GLA_HEAD_V:2 * GLA_DK + (h + 1) * GLA_HEAD_V]
            sbc_ref[c - 1, h] = _gla_bwd_state_step(sbc_ref[c, h], k_h, v_h, cs_ref[rows, kl])
        return carry

    lax.fori_loop(0, nc - 1, bwd_body, 0)

    ii = lax.broadcasted_iota(jnp.int32, (q, q), 0)
    jj = lax.broadcasted_iota(jnp.int32, (q, q), 1)
    lower_inc = ii >= jj
    upper_inc = ii <= jj

    def chunk_body(c, carry):
        rows = pl.ds(pl.multiple_of(c * q, q), q)
        for h in range(GLA_N_HEADS):
            fl = slice(h * GLA_HEAD_K, (h + 1) * GLA_HEAD_K)
            bl = slice(GLA_DK + h * GLA_HEAD_K, GLA_DK + (h + 1) * GLA_HEAD_K)
            q_h = qkvg_ref[rows, fl].astype(F32) * scale
            k_h = qkvg_ref[rows, bl].astype(F32)
            v_h = qkvg_ref[rows, 2 * GLA_DK + h * GLA_HEAD_V:2 * GLA_DK + (h + 1) * GLA_HEAD_V]
            csf = cs_ref[rows, fl]
            csb = cs_ref[rows, bl]
            qf = (q_h * jnp.exp(csf)).astype(BF16)
            kf = (k_h * jnp.exp(-csf)).astype(BF16)
            qb = (q_h * jnp.exp(csb)).astype(BF16)
            kb = (k_h * jnp.exp(-csb)).astype(BF16)
            att = (jnp.where(lower_inc, _dot_nt(qf, kf), 0.0)
                   + jnp.where(upper_inc, _dot_nt(qb, kb), 0.0))
            sf = sf_ref[h]
            s_cat = jnp.concatenate([sf, sbc_ref[c, h]], axis=1).astype(BF16)
            o = _dot(att.astype(BF16), v_h) + _dot_nt(jnp.concatenate([qf, qb], axis=1), s_cat)
            oacc_ref[rows, h * GLA_HEAD_V:(h + 1) * GLA_HEAD_V] = o
            tot = csf[q - 1:q, :]
            k_dec = (k_h * jnp.exp(tot - csf)).astype(BF16)
            sf_ref[h] = jnp.exp(tot) * sf + _dot_tn(v_h, k_dec)
        return carry

    lax.fori_loop(0, nc, chunk_body, 0)

    g = qkvg_ref[:, 2 * GLA_DK + GLA_DV:2 * GLA_DK + 2 * GLA_DV].astype(F32)
    o = oacc_ref[...]
    outs = []
    for h in range(GLA_N_HEADS):
        hl = slice(h * GLA_HEAD_V, (h + 1) * GLA_HEAD_V)
        oh = o[:, hl]
        ms = jnp.mean(oh * oh, axis=-1, keepdims=True)
        outs.append((oh * lax.rsqrt(ms + EPS) * normw_ref[:, hl] * _silu(g[:, hl])).astype(BF16))
    o_ref[...] = _dot(jnp.concatenate(outs, axis=1), wout_ref[...]).astype(o_ref.dtype)


def _gla(big, small, batch, seq_len, wts):
    nt = batch * seq_len
    tile = GLA_TILE
    n_tiles = seq_len // tile
    nc = tile // GLA_CHUNK
    qkvg_w = 2 * GLA_DK + 2 * GLA_DV
    qkvg_block = SSD_XBC // qkvg_w
    rev = lambda t: n_tiles - 1 - t
    consts_pre = [wts["gla_w2"], wts["gla_gb"], wts["gla_tri"], wts["gla_ones"]]
    state_shape = (GLA_N_HEADS, GLA_HEAD_V, GLA_HEAD_K)
    sb_in = pl.pallas_call(
        _gla_pre_kernel,
        grid=(batch, n_tiles),
        in_specs=[pl.BlockSpec((tile, qkvg_w), lambda b, t: (b * n_tiles + rev(t), qkvg_block)),
                  pl.BlockSpec((tile, SMALL_W), lambda b, t: (b * n_tiles + rev(t), 0))]
        + [_const_spec(c) for c in consts_pre],
        out_specs=pl.BlockSpec((1, 1) + state_shape, lambda b, t: (b, rev(t), 0, 0, 0)),
        out_shape=jax.ShapeDtypeStruct((batch, n_tiles) + state_shape, F32),
        scratch_shapes=[pltpu.VMEM((tile, GLA_DK), F32),
                        pltpu.VMEM(state_shape, F32)],
        compiler_params=_cparams(2),
        name="gla_pre",
    )(big, small, *consts_pre)

    consts = consts_pre + [wts["gla_normw"], wts["w_gla_out"]]
    return pl.pallas_call(
        _gla_main_kernel,
        grid=(batch, n_tiles),
        in_specs=[pl.BlockSpec((tile, qkvg_w), lambda b, t: (b * n_tiles + t, qkvg_block)),
                  pl.BlockSpec((tile, SMALL_W), lambda b, t: (b * n_tiles + t, 0)),
                  pl.BlockSpec((1, 1) + state_shape, lambda b, t: (b, t, 0, 0, 0))]
        + [_const_spec(c) for c in consts],
        out_specs=pl.BlockSpec((tile, D_MODEL), lambda b, t: (b * n_tiles + t, 0)),
        out_shape=jax.ShapeDtypeStruct((nt, D_MODEL), F32),
        scratch_shapes=[pltpu.VMEM((tile, 2 * GLA_DK), F32),
                        pltpu.VMEM(state_shape, F32),
                        pltpu.VMEM((nc,) + state_shape, F32),
                        pltpu.VMEM((tile, GLA_DV), F32)],
        compiler_params=_cparams(2),
        name="gla_main",
    )(big, small, sb_in, *consts)


def _merge_kernel(x_ref, a_ref, b_ref, ga_ref, gb_ref, wo_ref, o_ref):
    mixed = (_sigmoid(ga_ref[...].astype(F32)) * a_ref[...]
             + _sigmoid(gb_ref[...].astype(F32)) * b_ref[...])
    o_ref[...] = x_ref[...] + _dot(mixed.astype(BF16), wo_ref[...])


def _merge(x, br_a, br_b, big, w_o):
    nt = x.shape[0]
    tm = min(PROJ_TM, nt)
    gate_block = (BIG_W - 2 * D_MODEL) // D_MODEL
    row = lambda i: (i, 0)
    return pl.pallas_call(
        _merge_kernel,
        grid=(nt // tm,),
        in_specs=[pl.BlockSpec((tm, D_MODEL), row),
                  pl.BlockSpec((tm, D_MODEL), row),
                  pl.BlockSpec((tm, D_MODEL), row),
                  pl.BlockSpec((tm, D_MODEL), lambda i: (i, gate_block)),
                  pl.BlockSpec((tm, D_MODEL), lambda i: (i, gate_block + 1)),
                  pl.BlockSpec((D_MODEL, D_MODEL), lambda i: (0, 0))],
        out_specs=pl.BlockSpec((tm, D_MODEL), row),
        out_shape=jax.ShapeDtypeStruct((nt, D_MODEL), F32),
        compiler_params=_cparams(1),
        name="merge",
    )(x, br_a, br_b, big, big, w_o)


def _ffn_kernel(prev_ref, cur_ref, next_ref, n2_ref, wg_ref, wv_ref, cwg_ref, cwv_ref, cbg_ref, cbv_ref,
                wd_ref, fn_ref, o_ref, h_ref, ug_ref, uv_ref, acc_ref, *, n_tiles):
    t = pl.program_id(1)
    f = pl.program_id(2)
    nf = pl.num_programs(2)
    tile = FFN_TILE

    def norm_rows(x):
        ms = jnp.mean(x * x, axis=-1, keepdims=True)
        return x * lax.rsqrt(ms + EPS) * n2_ref[...]

    @pl.when(f == 0)
    def _():
        h_ref[0:HALO, :] = norm_rows(prev_ref[...])
        h_ref[HALO:HALO + tile, :] = norm_rows(cur_ref[...])
        h_ref[HALO + tile:HALO + tile + HALO, :] = norm_rows(next_ref[...])
        acc_ref[...] = jnp.zeros_like(acc_ref)

    h = h_ref[...].astype(BF16)
    row = lax.broadcasted_iota(jnp.int32, (tile + 2 * HALO, 1), 0)
    keep = jnp.logical_and(jnp.logical_or(row >= HALO, t > 0),
                           jnp.logical_or(row < HALO + tile, t < n_tiles - 1))
    ug_ref[...] = jnp.where(keep, _dot(h, wg_ref[...]), 0.0)
    uv_ref[...] = jnp.where(keep, _dot(h, wv_ref[...]), 0.0)

    def conv3(u_ref, cw_ref, cb_ref):
        acc = None
        for k in range(3):
            term = u_ref[HALO - 1 + k:HALO - 1 + k + tile, :] * cw_ref[k:k + 1, :]
            acc = term if acc is None else acc + term
        return acc + cb_ref[...]

    act = _silu(conv3(ug_ref, cwg_ref, cbg_ref)) * conv3(uv_ref, cwv_ref, cbv_ref)
    acc_ref[...] += _dot(act.astype(BF16), wd_ref[...])

    @pl.when(f == nf - 1)
    def _():
        x2 = cur_ref[...] + acc_ref[...]
        ms = jnp.mean(x2 * x2, axis=-1, keepdims=True)
        o_ref[...] = x2 * lax.rsqrt(ms + EPS) * fn_ref[...]


def _ffn(x1, batch, seq_len, wts):
    nt = batch * seq_len
    tile = FFN_TILE
    n_tiles = seq_len // tile
    nf = D_FF // FFN_F
    rb = tile // HALO
    seq_rb = seq_len // HALO
    last_rb = nt // HALO - 1
    c2 = lambda b, t, f: (0, 0)
    return pl.pallas_call(
        functools.partial(_ffn_kernel, n_tiles=n_tiles),
        grid=(batch, n_tiles, nf),
        in_specs=[pl.BlockSpec((HALO, D_MODEL), lambda b, t, f: (jnp.maximum(b * seq_rb + t * rb - 1, 0), 0)),
                  pl.BlockSpec((tile, D_MODEL), lambda b, t, f: (b * n_tiles + t, 0)),
                  pl.BlockSpec((HALO, D_MODEL),
                               lambda b, t, f: (jnp.minimum(b * seq_rb + (t + 1) * rb, last_rb), 0)),
                  pl.BlockSpec((1, D_MODEL), c2),
                  pl.BlockSpec((D_MODEL, FFN_F), lambda b, t, f: (0, f)),
                  pl.BlockSpec((D_MODEL, FFN_F), lambda b, t, f: (0, nf + f)),
                  pl.BlockSpec((3, FFN_F), lambda b, t, f: (0, f)),
                  pl.BlockSpec((3, FFN_F), lambda b, t, f: (0, nf + f)),
                  pl.BlockSpec((1, FFN_F), lambda b, t, f: (0, f)),
                  pl.BlockSpec((1, FFN_F), lambda b, t, f: (0, nf + f)),
                  pl.BlockSpec((FFN_F, D_MODEL), lambda b, t, f: (f, 0)),
                  pl.BlockSpec((1, D_MODEL), c2)],
        out_specs=pl.BlockSpec((tile, D_MODEL), lambda b, t, f: (b * n_tiles + t, 0)),
        out_shape=jax.ShapeDtypeStruct((nt, D_MODEL), F32),
        scratch_shapes=[pltpu.VMEM((tile + 2 * HALO, D_MODEL), F32),
                        pltpu.VMEM((tile + 2 * HALO, FFN_F), F32),
                        pltpu.VMEM((tile + 2 * HALO, FFN_F), F32),
                        pltpu.VMEM((tile, D_MODEL), F32)],
        compiler_params=_cparams(3),
        name="ffn",
    )(x1, x1, x1, wts["norm2_w"], wts["w_ffn_up"], wts["w_ffn_up"], wts["ffn_conv_w"], wts["ffn_conv_w"],
      wts["ffn_conv_b"], wts["ffn_conv_b"], wts["w_ffn_down"], wts["final_norm_w"])


def _block_tri(n, blk, inclusive_lower):
    i = np.arange(n)[:, None]
    j = np.arange(n)[None, :]
    same = (i // blk) == (j // blk)
    m = same & (i >= j) if inclusive_lower else same
    return jnp.asarray(m, dtype=BF16)


def _prep_weights(norm1_w, w_in, ssd_conv_w, ssd_conv_b, ssd_a_log, ssd_dt_bias, ssd_d, ssd_norm_w, w_ssd_out,
                  gla_gate_w2, gla_gate_b, gla_norm_w, w_gla_out, w_o, norm2_w, w_ffn_up, ffn_conv_w, ffn_conv_b,
                  w_ffn_down, final_norm_w):
    o_z = 0
    o_xbc = o_z + SSD_D_INNER
    o_dt = o_xbc + SSD_XBC
    o_q = o_dt + 2 * SSD_N_HEADS
    o_k = o_q + GLA_DK
    o_v = o_k + GLA_DK
    o_g = o_v + GLA_DV
    o_gk = o_g + GLA_DV
    o_gate = o_gk + 2 * GLA_GATE_RANK
    w_big = jnp.concatenate([w_in[:, o_xbc:o_dt], w_in[:, o_q:o_gk], w_in[:, o_z:o_xbc], w_in[:, o_gate:]],
                            axis=1).astype(BF16)
    pad = SMALL_W - 2 * SSD_N_HEADS - 2 * GLA_GATE_RANK
    w_small = jnp.concatenate([w_in[:, o_dt:o_q], w_in[:, o_gk:o_gate], jnp.zeros((D_MODEL, pad), F32)],
                              axis=1).astype(BF16)

    def row128(v):
        return jnp.concatenate([v.reshape(-1), jnp.zeros((SMALL_W - v.size,), F32)]).reshape(1, SMALL_W)

    head_of_lane = np.arange(SSD_D_INNER) // SSD_HEAD_DIM
    ef = (np.arange(SMALL_W)[:, None] == head_of_lane[None, :])
    eb = (np.arange(SMALL_W)[:, None] == (head_of_lane[None, :] + SSD_N_HEADS))
    w2 = jnp.zeros((SMALL_W, 2 * GLA_DK), F32)
    w2 = w2.at[GK_OFF:GK_OFF + GLA_GATE_RANK, :GLA_DK].set(gla_gate_w2[0])
    w2 = w2.at[GK_OFF + GLA_GATE_RANK:GK_OFF + 2 * GLA_GATE_RANK, GLA_DK:].set(gla_gate_w2[1])
    return {
        "norm1_w": norm1_w.reshape(1, D_MODEL),
        "w_big": w_big,
        "w_small": w_small,
        "ssd_conv_w": ssd_conv_w,
        "ssd_conv_b": ssd_conv_b.reshape(1, SSD_XBC),
        "ssd_bias": row128(ssd_dt_bias),
        "ssd_nega": row128(-jnp.exp(ssd_a_log)),
        "ssd_tri": _block_tri(SSD_TILE, SSD_Q, True),
        "ssd_ones": _block_tri(SSD_TILE, SSD_Q, False),
        "ssd_tri_full": _block_tri(SSD_TILE, SSD_TILE, True),
        "ssd_ones_full": _block_tri(SSD_TILE, SSD_TILE, False),
        "ssd_ef2": jnp.asarray(np.concatenate([ef, ef], axis=0), dtype=BF16),
        "ssd_eb2": jnp.asarray(np.concatenate([eb, eb], axis=0), dtype=BF16),
        "ssd_drow": jnp.repeat(ssd_d, SSD_HEAD_DIM).reshape(1, SSD_D_INNER),
        "ssd_normw": ssd_norm_w.reshape(1, SSD_D_INNER),
        "w_ssd_out": w_ssd_out.astype(BF16),
        "gla_w2": w2.astype(BF16),
        "gla_gb": gla_gate_b.reshape(1, 2 * GLA_DK),
        "gla_tri": _block_tri(GLA_TILE, GLA_CHUNK, True),
        "gla_ones": _block_tri(GLA_TILE, GLA_CHUNK, False),
        "gla_normw": gla_norm_w.reshape(1, GLA_DV),
        "w_gla_out": w_gla_out.astype(BF16),
        "w_o": w_o.astype(BF16),
        "norm2_w": norm2_w.reshape(1, D_MODEL),
        "w_ffn_up": w_ffn_up.astype(BF16),
        "ffn_conv_w": ffn_conv_w,
        "ffn_conv_b": ffn_conv_b.reshape(1, 2 * D_FF),
        "w_ffn_down": w_ffn_down.astype(BF16),
        "final_norm_w": final_norm_w.reshape(1, D_MODEL),
    }


def _trunk(x3, wts):
    batch, seq_len, _ = x3.shape
    x = x3.reshape(batch * seq_len, D_MODEL)
    big = _norm_proj(x, wts["norm1_w"], wts["w_big"], BF16, PROJ_TN)
    small = _norm_proj(x, wts["norm1_w"], wts["w_small"], F32, SMALL_W)
    br_a = _ssd(big, small, batch, seq_len, wts)
    br_b = _gla(big, small, batch, seq_len, wts)
    x1 = _merge(x, br_a, br_b, big, wts["w_o"])
    out = _ffn(x1, batch, seq_len, wts)
    return out.reshape(batch, seq_len, D_MODEL)


def kernel(x_prompt, x_sample, norm1_w, w_in, ssd_conv_w, ssd_conv_b, ssd_a_log, ssd_dt_bias, ssd_d, ssd_norm_w,
           w_ssd_out, gla_gate_w2, gla_gate_b, gla_norm_w, w_gla_out, w_o, norm2_w, w_ffn_up, ffn_conv_w,
           ffn_conv_b, w_ffn_down, final_norm_w):
    wts = _prep_weights(norm1_w[0], w_in[0], ssd_conv_w[0], ssd_conv_b[0], ssd_a_log[0], ssd_dt_bias[0], ssd_d[0],
                        ssd_norm_w[0], w_ssd_out[0], gla_gate_w2[0], gla_gate_b[0], gla_norm_w[0], w_gla_out[0],
                        w_o[0], norm2_w[0], w_ffn_up[0], ffn_conv_w[0], ffn_conv_b[0], w_ffn_down[0], final_norm_w)
    return (_trunk(x_prompt, wts), _trunk(x_sample, wts))
```

```python
import functools

import numpy as np
import jax
import jax.numpy as jnp
from jax import lax
from jax.experimental import pallas as pl
from jax.experimental.pallas import tpu as pltpu

F32 = jnp.float32
BF16 = jnp.bfloat16

D_MODEL = 1024
SSD_D_INNER = 2048
SSD_HEAD_DIM = 64
SSD_N_HEADS = 32
SSD_N_GROUPS = 4
SSD_D_STATE = 128
SSD_GROUP_W = SSD_D_INNER // SSD_N_GROUPS
SSD_BC_W = SSD_N_GROUPS * SSD_D_STATE
SSD_XBC = SSD_D_INNER + 2 * SSD_BC_W
SSD_CONV = 5
GLA_N_HEADS = 4
GLA_DK = 512
GLA_DV = 1024
GLA_HEAD_K = 128
GLA_HEAD_V = 256
GLA_GATE_RANK = 16
GLA_GATE_NORM = 16.0
GLA_CHUNK = 64
D_FF = 2816
EPS = 1e-6

LANES = 128
SUBLANES = 8
VMEM_LIMIT = 56 * 1024 * 1024

BIG_W = SSD_XBC + 2 * GLA_DK + 2 * GLA_DV + SSD_D_INNER + 2 * D_MODEL
SMALL_W = LANES
GK_OFF = 2 * SSD_N_HEADS

SSD_TILE = 256
SSD_Q = 64
GLA_TILE = 256
HALO = SUBLANES
SSD_HALO = 2 * SUBLANES
CONV_SLAB = 512
FFN_TILE = 512
FFN_F = 1408
PROJ_TM = 1024
PROJ_TN = 1024


def _dot(a, b):
    return jnp.dot(a, b, preferred_element_type=F32)


def _dot_tn(a, b):
    return lax.dot_general(a, b, (((0,), (0,)), ((), ())), preferred_element_type=F32)


def _dot_nt(a, b):
    return lax.dot_general(a, b, (((1,), (1,)), ((), ())), preferred_element_type=F32)


def _split_bf16(x, n):
    parts = []
    r = x
    for i in range(n):
        p = r.astype(BF16)
        parts.append(p)
        if i + 1 < n:
            r = r - p.astype(F32)
    return parts


def _dot_exact_lhs(m_bf16, x, n=3):
    acc = None
    for p in _split_bf16(x, n):
        t = _dot(m_bf16, p)
        acc = t if acc is None else acc + t
    return acc


def _sigmoid(x):
    return 0.5 * jnp.tanh(0.5 * x) + 0.5


def _silu(x):
    return x * _sigmoid(x)


def _softplus(x):
    return jnp.maximum(x, 0.0) + jnp.log1p(jnp.exp(-jnp.abs(x)))


def _cparams(n_axes):
    return pltpu.CompilerParams(dimension_semantics=("arbitrary",) * n_axes,
                                vmem_limit_bytes=VMEM_LIMIT)


def _proj_kernel(x_ref, nw_ref, w_ref, ws_ref, o_ref, os_ref, h_ref):
    @pl.when(pl.program_id(1) == 0)
    def _():
        x = x_ref[...]
        ms = jnp.mean(x * x, axis=-1, keepdims=True)
        h = (x * lax.rsqrt(ms + EPS) * nw_ref[...]).astype(BF16)
        h_ref[...] = h
        os_ref[...] = _dot(h, ws_ref[...])

    o_ref[...] = _dot(h_ref[...], w_ref[...]).astype(o_ref.dtype)


def _norm_proj(x, nw, w_big, w_small):
    nt, d = x.shape
    n = w_big.shape[1]
    tm = min(PROJ_TM, nt)
    tn = PROJ_TN
    return pl.pallas_call(
        _proj_kernel,
        grid=(nt // tm, n // tn),
        in_specs=[pl.BlockSpec((tm, d), lambda i, j: (i, 0)),
                  pl.BlockSpec((1, d), lambda i, j: (0, 0)),
                  pl.BlockSpec((d, tn), lambda i, j: (0, j)),
                  pl.BlockSpec((d, SMALL_W), lambda i, j: (0, 0))],
        out_specs=[pl.BlockSpec((tm, tn), lambda i, j: (i, j)),
                   pl.BlockSpec((tm, SMALL_W), lambda i, j: (i, 0))],
        out_shape=[jax.ShapeDtypeStruct((nt, n), BF16),
                   jax.ShapeDtypeStruct((nt, SMALL_W), F32)],
        scratch_shapes=[pltpu.VMEM((tm, d), BF16)],
        compiler_params=_cparams(2),
        name="norm_proj",
    )(x, nw, w_big, w_small)


def _fill_ext(ext_ref, prev_ref, cur_ref, next_ref, first, last, tile):
    halo = SSD_HALO
    ext_ref[0:halo, :] = jnp.where(first, 0.0, prev_ref[...].astype(F32))
    ext_ref[halo:halo + tile, :] = cur_ref[...].astype(F32)
    ext_ref[halo + tile:halo + tile + halo, :] = jnp.where(last, 0.0, next_ref[...].astype(F32))


def _conv_taps(ext_ref, w_ref, b_ref, tile, lo, hi, n_taps):
    pad = (n_taps - 1) // 2
    acc = None
    for k in range(n_taps):
        xk = ext_ref[SSD_HALO - pad + k:SSD_HALO - pad + k + tile, lo:hi]
        t = xk * w_ref[k:k + 1, lo:hi]
        acc = t if acc is None else acc + t
    return acc + b_ref[:, lo:hi]


def _ssd_small(small_ref, bias_ref, nega_ref, tri_ref, ones_ref):
    s = small_ref[...]
    dt = _softplus(s + bias_ref[...])
    a = dt * nega_ref[...]
    p = _dot_exact_lhs(tri_ref[...], a)
    tot = _dot_exact_lhs(ones_ref[...], a)
    col = lax.broadcasted_iota(jnp.int32, s.shape, 1)
    fwd = col < SSD_N_HEADS
    r = tot - p + a
    pr = jnp.where(fwd, p, r)
    w = dt * jnp.exp(jnp.where(fwd, tot - p, p - a))
    rowarg = jnp.log(dt) - pr
    dsum = pltpu.roll(dt, 2 * SSD_N_HEADS, 1) + pltpu.roll(dt, SSD_N_HEADS, 1)
    diag_cols = (col >= 2 * SSD_N_HEADS) & (col < 3 * SSD_N_HEADS)
    rowarg = jnp.where(diag_cols, jnp.log(dsum), rowarg)
    return pr, w, rowarg


def _expand2(x, e2_ref):
    hi, lo = _split_bf16(x, 2)
    return _dot(jnp.concatenate([hi, lo], axis=1), e2_ref[...])


def _ssd_bwd_state_step(sb, xs_c, b_c, wb_c, decb_row):
    out = []
    for g in range(SSD_N_GROUPS):
        gl = slice(g * SSD_GROUP_W, (g + 1) * SSD_GROUP_W)
        xd = xs_c[:, gl] * wb_c[:, gl]
        bg = b_c[:, g * SSD_D_STATE:(g + 1) * SSD_D_STATE]
        out.append(decb_row[:, gl] * sb[g] + _dot_tn(bg, xd))
    return out


def _ssd_pre_kernel(prev_ref, cur_ref, next_ref, small_ref, convw_ref, convb_ref, bias_ref, nega_ref,
                    tri_ref, ones_ref, eb2_ref, sbin_ref, xconv_ref,
                    ext_ref, sb_ref, *, n_tiles):
    t = pl.program_id(1)
    tile = SSD_TILE

    @pl.when(t == 0)
    def _():
        sb_ref[...] = jnp.zeros_like(sb_ref)

    sbin_ref[0, 0] = sb_ref[...]

    _fill_ext(ext_ref, prev_ref, cur_ref, next_ref, t == n_tiles - 1, t == 0, tile)
    for lo in range(0, SSD_XBC, CONV_SLAB):
        xs = _silu(_conv_taps(ext_ref, convw_ref, convb_ref, tile, lo, lo + CONV_SLAB, SSD_CONV))
        xconv_ref[:, lo:lo + CONV_SLAB] = xs.astype(BF16)

    pr, w, _ = _ssd_small(small_ref, bias_ref, nega_ref, tri_ref, ones_ref)
    wb = _dot(w.astype(BF16), eb2_ref[0:LANES, :]).astype(BF16)
    decb = _expand2(jnp.exp(pr[0:SUBLANES, :]), eb2_ref)[0:1, :]
    sb = [sb_ref[g] for g in range(SSD_N_GROUPS)]
    new = _ssd_bwd_state_step(sb, xconv_ref[:, 0:SSD_D_INNER],
                              xconv_ref[:, SSD_D_INNER:SSD_D_INNER + SSD_BC_W], wb, decb)
    for g in range(SSD_N_GROUPS):
        sb_ref[g] = new[g]


def _ssd_main_kernel(xc_ref, z_ref, ga_ref, small_ref, sbin_ref,
                     bias_ref, nega_ref, tri_ref, ones_ref, ef2_ref, eb2_ref,
                     drow_ref, normw_ref, wout_ref, o_ref,
                     rowarg_ref, prf_ref, prb_ref, wf_ref, wb_ref, exf_ref, exb_ref,
                     sf_ref, sbc_ref, y_ref):
    t = pl.program_id(1)
    tile = SSD_TILE
    q = SSD_Q
    nc = tile // q
    b_off = SSD_D_INNER
    c_off = SSD_D_INNER + SSD_BC_W

    @pl.when(t == 0)
    def _():
        sf_ref[...] = jnp.zeros_like(sf_ref)

    pr, w, rowarg = _ssd_small(small_ref, bias_ref, nega_ref, tri_ref, ones_ref)
    rowarg_ref[...] = rowarg
    wbf = w.astype(BF16)
    wf_ref[...] = _dot(wbf, ef2_ref[0:LANES, :]).astype(BF16)
    wb_ref[...] = _dot(wbf, eb2_ref[0:LANES, :]).astype(BF16)
    prf = _expand2(pr, ef2_ref)
    prf_ref[...] = prf
    exf_ref[...] = jnp.exp(prf)
    prb = _expand2(pr, eb2_ref)
    prb_ref[...] = prb
    exb_ref[...] = jnp.exp(prb)

    sbc_ref[nc - 1] = sbin_ref[0, 0]

    def bwd_body(k, carry):
        c = nc - 1 - k
        r0 = pl.multiple_of(c * q, q)
        rows = pl.ds(r0, q)
        sb = [sbc_ref[c, g] for g in range(SSD_N_GROUPS)]
        new = _ssd_bwd_state_step(sb, xc_ref[rows, 0:SSD_D_INNER], xc_ref[rows, b_off:c_off],
                                  wb_ref[rows, :], exb_ref[pl.ds(r0, 1), :])
        for g in range(SSD_N_GROUPS):
            sbc_ref[c - 1, g] = new[g]
        return carry

    lax.fori_loop(0, nc - 1, bwd_body, 0)

    ii = lax.broadcasted_iota(jnp.int32, (q, LANES), 0)
    ll = lax.broadcasted_iota(jnp.int32, (q, LANES), 1)
    jj = jnp.where(ll < SSD_HEAD_DIM, ll, ll - SSD_HEAD_DIM)
    lower = ii > jj
    upper = ii < jj
    left = ll < SSD_HEAD_DIM
    left_row = lax.broadcasted_iota(jnp.int32, (1, LANES), 1) < SSD_HEAD_DIM
    hpg = SSD_N_HEADS // SSD_N_GROUPS

    def pair_body(k, carry):
        k0 = pl.multiple_of(k * 2 * q, 2 * q)
        row_t = rowarg_ref[pl.ds(k0, 2 * q), :].T
        row_tr = pltpu.roll(row_t, q, 1)
        for s in range(2):
            r0 = pl.multiple_of(k0 + s * q, q)
            rows = pl.ds(r0, q)
            c = 2 * k + s
            lo_src, hi_src = (row_t, row_tr) if s == 0 else (row_tr, row_t)

            def pair_row(base, h0):
                return jnp.where(left_row, lo_src[base + h0:base + h0 + 1, :],
                                 hi_src[base + h0 + 1:base + h0 + 2, :])

            for g in range(SSD_N_GROUPS):
                gl = slice(g * SSD_GROUP_W, (g + 1) * SSD_GROUP_W)
                b_g = xc_ref[rows, b_off + g * SSD_D_STATE:b_off + (g + 1) * SSD_D_STATE]
                c_g = xc_ref[rows, c_off + g * SSD_D_STATE:c_off + (g + 1) * SSD_D_STATE]
                cb2 = _dot_nt(c_g, jnp.concatenate([b_g, b_g], axis=0))
                sf = sf_ref[g]
                y_off = (_dot(c_g, sf.astype(BF16)) * exf_ref[rows, gl]
                         + _dot(c_g, sbc_ref[c, g].astype(BF16)) * exb_ref[rows, gl])
                xs_g = xc_ref[rows, gl].astype(F32)
                for pm in range(hpg // 2):
                    h0 = g * hpg + 2 * pm
                    cl = slice(g * SSD_GROUP_W + pm * LANES, g * SSD_GROUP_W + (pm + 1) * LANES)
                    arg = jnp.where(
                        lower, prf_ref[rows, cl] + pair_row(0, h0),
                        jnp.where(upper, prb_ref[rows, cl] + pair_row(SSD_N_HEADS, h0),
                                  pair_row(2 * SSD_N_HEADS, h0)))
                    m2 = (cb2 * jnp.exp(arg)).astype(BF16)
                    x_pair = xs_g[:, pm * LANES:(pm + 1) * LANES]
                    x2 = jnp.concatenate([jnp.where(left, x_pair, 0.0), jnp.where(left, 0.0, x_pair)],
                                         axis=0).astype(BF16)
                    y_ref[rows, cl] = (_dot(m2, x2) + y_off[:, pm * LANES:(pm + 1) * LANES]
                                       + drow_ref[:, cl] * x_pair)
                xd = xc_ref[rows, gl] * wf_ref[rows, gl]
                decf = exf_ref[pl.ds(r0 + q - 1, 1), gl]
                sf_ref[g] = decf * sf + _dot_tn(b_g, xd)
        return carry

    lax.fori_loop(0, nc // 2, pair_body, 0)

    z = z_ref[...].astype(F32)
    y = y_ref[...] * _silu(z)
    outs = []
    for g in range(SSD_N_GROUPS):
        gl = slice(g * SSD_GROUP_W, (g + 1) * SSD_GROUP_W)
        yg = y[:, gl]
        ms = jnp.mean(yg * yg, axis=-1, keepdims=True)
        outs.append((yg * lax.rsqrt(ms + EPS) * normw_ref[:, gl]).astype(BF16))
    yn = jnp.concatenate(outs, axis=1)
    o_ref[...] = (_sigmoid(ga_ref[...].astype(F32)) * _dot(yn, wout_ref[...])).astype(o_ref.dtype)


def _const_spec(arr):
    nd = arr.ndim
    return pl.BlockSpec(arr.shape, lambda b, t: (0,) * nd)


def _ssd(big, small, batch, seq_len, wts):
    nt = batch * seq_len
    tile = SSD_TILE
    n_tiles = seq_len // tile
    nc = tile // SSD_Q
    rb = tile // SSD_HALO
    seq_rb = seq_len // SSD_HALO
    last_rb = nt // SSD_HALO - 1
    state_shape = (SSD_N_GROUPS, SSD_D_STATE, SSD_GROUP_W)
    rev = lambda t: n_tiles - 1 - t
    consts_pre = [wts["ssd_conv_w"], wts["ssd_conv_b"], wts["ssd_bias"], wts["ssd_nega"],
                  wts["ssd_tri_full"], wts["ssd_ones_full"], wts["ssd_eb2"]]
    sb_in, xconv = pl.pallas_call(
        functools.partial(_ssd_pre_kernel, n_tiles=n_tiles),
        grid=(batch, n_tiles),
        in_specs=[pl.BlockSpec((SSD_HALO, SSD_XBC),
                               lambda b, t: (jnp.maximum(b * seq_rb + rev(t) * rb - 1, 0), 0)),
                  pl.BlockSpec((tile, SSD_XBC), lambda b, t: (b * n_tiles + rev(t), 0)),
                  pl.BlockSpec((SSD_HALO, SSD_XBC),
                               lambda b, t: (jnp.minimum(b * seq_rb + (rev(t) + 1) * rb, last_rb), 0)),
                  pl.BlockSpec((tile, SMALL_W), lambda b, t: (b * n_tiles + rev(t), 0))]
        + [_const_spec(c) for c in consts_pre],
        out_specs=[pl.BlockSpec((1, 1) + state_shape, lambda b, t: (b, rev(t), 0, 0, 0)),
                   pl.BlockSpec((tile, SSD_XBC), lambda b, t: (b * n_tiles + rev(t), 0))],
        out_shape=[jax.ShapeDtypeStruct((batch, n_tiles) + state_shape, F32),
                   jax.ShapeDtypeStruct((nt, SSD_XBC), BF16)],
        scratch_shapes=[pltpu.VMEM((tile + 2 * SSD_HALO, SSD_XBC), F32),
                        pltpu.VMEM(state_shape, F32)],
        compiler_params=_cparams(2),
        name="ssd_pre",
    )(big, big, big, small, *consts_pre)

    consts = [wts["ssd_bias"], wts["ssd_nega"], wts["ssd_tri"], wts["ssd_ones"], wts["ssd_ef2"], wts["ssd_eb2"],
              wts["ssd_drow"], wts["ssd_normw"], wts["w_ssd_out"]]
    z_block = (SSD_XBC + 2 * GLA_DK + 2 * GLA_DV) // SSD_D_INNER
    gate_a_block = (BIG_W - 2 * D_MODEL) // D_MODEL
    wide = pltpu.VMEM((tile, SSD_D_INNER), F32)
    wide16 = pltpu.VMEM((tile, SSD_D_INNER), BF16)
    return pl.pallas_call(
        _ssd_main_kernel,
        grid=(batch, n_tiles),
        in_specs=[pl.BlockSpec((tile, SSD_XBC), lambda b, t: (b * n_tiles + t, 0)),
                  pl.BlockSpec((tile, SSD_D_INNER), lambda b, t: (b * n_tiles + t, z_block)),
                  pl.BlockSpec((tile, D_MODEL), lambda b, t: (b * n_tiles + t, gate_a_block)),
                  pl.BlockSpec((tile, SMALL_W), lambda b, t: (b * n_tiles + t, 0)),
                  pl.BlockSpec((1, 1) + state_shape, lambda b, t: (b, t, 0, 0, 0))]
        + [_const_spec(c) for c in consts],
        out_specs=pl.BlockSpec((tile, D_MODEL), lambda b, t: (b * n_tiles + t, 0)),
        out_shape=jax.ShapeDtypeStruct((nt, D_MODEL), BF16),
        scratch_shapes=[pltpu.VMEM((tile, SMALL_W), F32),
                        wide, wide, wide16, wide16, wide, wide,
                        pltpu.VMEM(state_shape, F32),
                        pltpu.VMEM((nc,) + state_shape, F32),
                        wide],
        compiler_params=_cparams(2),
        name="ssd_main",
    )(xconv, big, big, small, sb_in, *consts)


def _gla_log_alpha(small_ref, w2, gb):
    pre = _dot(small_ref[...].astype(BF16), w2) + gb
    return -(jnp.maximum(-pre, 0.0) + jnp.log(1.0 + jnp.exp(-jnp.abs(pre)))) * (1.0 / GLA_GATE_NORM)


def _gla_bwd_state_step(sb_h, k_h, v_h, cs_h):
    tot = cs_h[0:1, :]
    k_dec = (k_h * jnp.exp(tot - cs_h)).astype(BF16)
    return jnp.exp(tot) * sb_h + _dot_tn(v_h, k_dec)


def _gla_pre_kernel(qkvg_ref, small_ref, w2_ref, gb_ref, triu_ref, sbin_ref, sb_ref):
    t = pl.program_id(1)

    @pl.when(t == 0)
    def _():
        sb_ref[...] = jnp.zeros_like(sb_ref)

    sbin_ref[0, 0] = sb_ref[...]
    la = _gla_log_alpha(small_ref, w2_ref[:, GLA_DK:], gb_ref[:, GLA_DK:])
    cs = _dot_exact_lhs(triu_ref[...], la, 2)
    for h in range(GLA_N_HEADS):
        kl = slice(h * GLA_HEAD_K, (h + 1) * GLA_HEAD_K)
        k_h = qkvg_ref[:, GLA_DK + h * GLA_HEAD_K:GLA_DK + (h + 1) * GLA_HEAD_K].astype(F32)
        v_h = qkvg_ref[:, 2 * GLA_DK + h * GLA_HEAD_V:2 * GLA_DK + (h + 1) * GLA_HEAD_V]
        sb_ref[h] = _gla_bwd_state_step(sb_ref[h], k_h, v_h, cs[:, kl])


def _gla_main_kernel(qkvg_ref, small_ref, sbin_ref, x_ref, a_ref, gateb_ref,
                     w2_ref, gb_ref, tril_ref, triu_ref, normw_ref, wout_ref, wo_ref, o_ref,
                     cs_ref, sf_ref, sbc_ref, oacc_ref):
    t = pl.program_id(1)
    tile = GLA_TILE
    q = GLA_CHUNK
    nc = tile // q
    scale = GLA_HEAD_K ** -0.5

    @pl.when(t == 0)
    def _():
        sf_ref[...] = jnp.zeros_like(sf_ref)

    la = _gla_log_alpha(small_ref, w2_ref[...], gb_ref[...])
    cs_ref[:, 0:GLA_DK] = _dot_exact_lhs(tril_ref[...], la[:, 0:GLA_DK], 2)
    cs_ref[:, GLA_DK:2 * GLA_DK] = _dot_exact_lhs(triu_ref[...], la[:, GLA_DK:2 * GLA_DK], 2)

    sbc_ref[nc - 1] = sbin_ref[0, 0]

    def bwd_body(i, carry):
        c = nc - 1 - i
        rows = pl.ds(pl.multiple_of(c * q, q), q)
        for h in range(GLA_N_HEADS):
            kl = slice(GLA_DK + h * GLA_HEAD_K, GLA_DK + (h + 1) * GLA_HEAD_K)
            k_h = qkvg_ref[rows, kl].astype(F32)
            v_h = qkvg_ref[rows, 2 * GLA_DK + h * GLA_HEAD_V:2 * GLA_DK + (h + 1) * GLA_HEAD_V]
            sbc_ref[c - 1, h] = _gla_bwd_state_step(sbc_ref[c, h], k_h, v_h, cs_ref[rows, kl])
        return carry

    lax.fori_loop(0, nc - 1, bwd_body, 0)

    ii = lax.broadcasted_iota(jnp.int32, (q, q), 0)
    jj = lax.broadcasted_iota(jnp.int32, (q, q), 1)
    lower_inc = ii >= jj
    upper_inc = ii <= jj

    def chunk_body(c, carry):
        rows = pl.ds(pl.multiple_of(c * q, q), q)
        for h in range(GLA_N_HEADS):
            fl = slice(h * GLA_HEAD_K, (h + 1) * GLA_HEAD_K)
            bl = slice(GLA_DK + h * GLA_HEAD_K, GLA_DK + (h + 1) * GLA_HEAD_K)
            q_h = qkvg_ref[rows, fl].astype(F32) * scale
            k_h = qkvg_ref[rows, bl].astype(F32)
            v_h = qkvg_ref[rows, 2 * GLA_DK + h * GLA_HEAD_V:2 * GLA_DK + (h + 1) * GLA_HEAD_V]
            csf = cs_ref[rows, fl]
            csb = cs_ref[rows, bl]
            qf = (q_h * jnp.exp(csf)).astype(BF16)
            kf = (k_h * jnp.exp(-csf)).astype(BF16)
            qb = (q_h * jnp.exp(csb)).astype(BF16)
            kb = (k_h * jnp.exp(-csb)).astype(BF16)
            att = (jnp.where(lower_inc, _dot_nt(qf, kf), 0.0)
                   + jnp.where(upper_inc, _dot_nt(qb, kb), 0.0))
            sf = sf_ref[h]
            s_cat = jnp.concatenate([sf, sbc_ref[c, h]], axis=1).astype(BF16)
            o = _dot(att.astype(BF16), v_h) + _dot_nt(jnp.concatenate([qf, qb], axis=1), s_cat)
            oacc_ref[rows, h * GLA_HEAD_V:(h + 1) * GLA_HEAD_V] = o
            tot = csf[q - 1:q, :]
            k_dec = (k_h * jnp.exp(tot - csf)).astype(BF16)
            sf_ref[h] = jnp.exp(tot) * sf + _dot_tn(v_h, k_dec)
        return carry

    lax.fori_loop(0, nc, chunk_body, 0)

    g = qkvg_ref[:, 2 * GLA_DK + GLA_DV:2 * GLA_DK + 2 * GLA_DV].astype(F32)
    o = oacc_ref[...]
    outs = []
    for h in range(GLA_N_HEADS):
        hl = slice(h * GLA_HEAD_V, (h + 1) * GLA_HEAD_V)
        oh = o[:, hl]
        ms = jnp.mean(oh * oh, axis=-1, keepdims=True)
        outs.append((oh * lax.rsqrt(ms + EPS) * normw_ref[:, hl] * _silu(g[:, hl])).astype(BF16))
    branch_b = _dot(jnp.concatenate(outs, axis=1), wout_ref[...])
    mixed = a_ref[...].astype(F32) + _sigmoid(gateb_ref[...].astype(F32)) * branch_b
    o_ref[...] = x_ref[...] + _dot(mixed.astype(BF16), wo_ref[...])


def _gla(big, small, x, a_gated, batch, seq_len, wts):
    nt = batch * seq_len
    tile = GLA_TILE
    n_tiles = seq_len // tile
    nc = tile // GLA_CHUNK
    qkvg_w = 2 * GLA_DK + 2 * GLA_DV
    qkvg_block = SSD_XBC // qkvg_w
    gate_b_block = (BIG_W - D_MODEL) // D_MODEL
    rev = lambda t: n_tiles - 1 - t
    consts_pre = [wts["gla_w2"], wts["gla_gb"], wts["gla_triu_full"]]
    state_shape = (GLA_N_HEADS, GLA_HEAD_V, GLA_HEAD_K)
    sb_in = pl.pallas_call(
        _gla_pre_kernel,
        grid=(batch, n_tiles),
        in_specs=[pl.BlockSpec((tile, qkvg_w), lambda b, t: (b * n_tiles + rev(t), qkvg_block)),
                  pl.BlockSpec((tile, SMALL_W), lambda b, t: (b * n_tiles + rev(t), 0))]
        + [_const_spec(c) for c in consts_pre],
        out_specs=pl.BlockSpec((1, 1) + state_shape, lambda b, t: (b, rev(t), 0, 0, 0)),
        out_shape=jax.ShapeDtypeStruct((batch, n_tiles) + state_shape, F32),
        scratch_shapes=[pltpu.VMEM(state_shape, F32)],
        compiler_params=_cparams(2),
        name="gla_pre",
    )(big, small, *consts_pre)

    consts = [wts["gla_w2"], wts["gla_gb"], wts["gla_tril"], wts["gla_triu"], wts["gla_normw"],
              wts["w_gla_out"], wts["w_o"]]
    row = lambda b, t: (b * n_tiles + t, 0)
    return pl.pallas_call(
        _gla_main_kernel,
        grid=(batch, n_tiles),
        in_specs=[pl.BlockSpec((tile, qkvg_w), lambda b, t: (b * n_tiles + t, qkvg_block)),
                  pl.BlockSpec((tile, SMALL_W), row),
                  pl.BlockSpec((1, 1) + state_shape, lambda b, t: (b, t, 0, 0, 0)),
                  pl.BlockSpec((tile, D_MODEL), row),
                  pl.BlockSpec((tile, D_MODEL), row),
                  pl.BlockSpec((tile, D_MODEL), lambda b, t: (b * n_tiles + t, gate_b_block))]
        + [_const_spec(c) for c in consts],
        out_specs=pl.BlockSpec((tile, D_MODEL), row),
        out_shape=jax.ShapeDtypeStruct((nt, D_MODEL), F32),
        scratch_shapes=[pltpu.VMEM((tile, 2 * GLA_DK), F32),
                        pltpu.VMEM(state_shape, F32),
                        pltpu.VMEM((nc,) + state_shape, F32),
                        pltpu.VMEM((tile, GLA_DV), F32)],
        compiler_params=_cparams(2),
        name="gla_main",
    )(big, small, sb_in, x, a_gated, big, *consts)


def _ffn_kernel(prev_ref, cur_ref, next_ref, n2_ref, wg_ref, wv_ref, cwg_ref, cwv_ref, cbg_ref, cbv_ref,
                wd_ref, fn_ref, o_ref, h_ref, ug_ref, uv_ref, acc_ref, *, n_tiles):
    t = pl.program_id(1)
    f = pl.program_id(2)
    nf = pl.num_programs(2)
    tile = FFN_TILE

    def norm_rows(x):
        ms = jnp.mean(x * x, axis=-1, keepdims=True)
        return x * lax.rsqrt(ms + EPS) * n2_ref[...]

    @pl.when(f == 0)
    def _():
        h_ref[0:HALO, :] = norm_rows(prev_ref[...])
        h_ref[HALO:HALO + tile, :] = norm_rows(cur_ref[...])
        h_ref[HALO + tile:HALO + tile + HALO, :] = norm_rows(next_ref[...])
        acc_ref[...] = jnp.zeros_like(acc_ref)

    h = h_ref[...].astype(BF16)
    row = lax.broadcasted_iota(jnp.int32, (tile + 2 * HALO, 1), 0)
    keep = jnp.logical_and(jnp.logical_or(row >= HALO, t > 0),
                           jnp.logical_or(row < HALO + tile, t < n_tiles - 1))
    ug_ref[...] = jnp.where(keep, _dot(h, wg_ref[...]), 0.0)
    uv_ref[...] = jnp.where(keep, _dot(h, wv_ref[...]), 0.0)

    def conv3(u_ref, cw_ref, cb_ref):
        acc = None
        for k in range(3):
            term = u_ref[HALO - 1 + k:HALO - 1 + k + tile, :] * cw_ref[k:k + 1, :]
            acc = term if acc is None else acc + term
        return acc + cb_ref[...]

    act = _silu(conv3(ug_ref, cwg_ref, cbg_ref)) * conv3(uv_ref, cwv_ref, cbv_ref)
    acc_ref[...] += _dot(act.astype(BF16), wd_ref[...])

    @pl.when(f == nf - 1)
    def _():
        x2 = cur_ref[...] + acc_ref[...]
        ms = jnp.mean(x2 * x2, axis=-1, keepdims=True)
        o_ref[...] = x2 * lax.rsqrt(ms + EPS) * fn_ref[...]


def _ffn(x1, batch, seq_len, wts):
    nt = batch * seq_len
    tile = FFN_TILE
    n_tiles = seq_len // tile
    nf = D_FF // FFN_F
    rb = tile // HALO
    seq_rb = seq_len // HALO
    last_rb = nt // HALO - 1
    c2 = lambda b, t, f: (0, 0)
    return pl.pallas_call(
        functools.partial(_ffn_kernel, n_tiles=n_tiles),
        grid=(batch, n_tiles, nf),
        in_specs=[pl.BlockSpec((HALO, D_MODEL), lambda b, t, f: (jnp.maximum(b * seq_rb + t * rb - 1, 0), 0)),
                  pl.BlockSpec((tile, D_MODEL), lambda b, t, f: (b * n_tiles + t, 0)),
                  pl.BlockSpec((HALO, D_MODEL),
                               lambda b, t, f: (jnp.minimum(b * seq_rb + (t + 1) * rb, last_rb), 0)),
                  pl.BlockSpec((1, D_MODEL), c2),
                  pl.BlockSpec((D_MODEL, FFN_F), lambda b, t, f: (0, f)),
                  pl.BlockSpec((D_MODEL, FFN_F), lambda b, t, f: (0, nf + f)),
                  pl.BlockSpec((3, FFN_F), lambda b, t, f: (0, f)),
                  pl.BlockSpec((3, FFN_F), lambda b, t, f: (0, nf + f)),
                  pl.BlockSpec((1, FFN_F), lambda b, t, f: (0, f)),
                  pl.BlockSpec((1, FFN_F), lambda b, t, f: (0, nf + f)),
                  pl.BlockSpec((FFN_F, D_MODEL), lambda b, t, f: (f, 0)),
                  pl.BlockSpec((1, D_MODEL), c2)],
        out_specs=pl.BlockSpec((tile, D_MODEL), lambda b, t, f: (b * n_tiles + t, 0)),
        out_shape=jax.ShapeDtypeStruct((nt, D_MODEL), F32),
        scratch_shapes=[pltpu.VMEM((tile + 2 * HALO, D_MODEL), F32),
                        pltpu.VMEM((tile + 2 * HALO, FFN_F), F32),
                        pltpu.VMEM((tile + 2 * HALO, FFN_F), F32),
                        pltpu.VMEM((tile, D_MODEL), F32)],
        compiler_params=_cparams(3),
        name="ffn",
    )(x1, x1, x1, wts["norm2_w"], wts["w_ffn_up"], wts["w_ffn_up"], wts["ffn_conv_w"], wts["ffn_conv_w"],
      wts["ffn_conv_b"], wts["ffn_conv_b"], wts["w_ffn_down"], wts["final_norm_w"])


def _block_tri(n, blk, kind):
    i = np.arange(n)[:, None]
    j = np.arange(n)[None, :]
    m = (i // blk) == (j // blk)
    if kind == "lower":
        m = m & (i >= j)
    elif kind == "upper":
        m = m & (i <= j)
    return jnp.asarray(m, dtype=BF16)


def _prep_weights(norm1_w, w_in, ssd_conv_w, ssd_conv_b, ssd_a_log, ssd_dt_bias, ssd_d, ssd_norm_w, w_ssd_out,
                  gla_gate_w2, gla_gate_b, gla_norm_w, w_gla_out, w_o, norm2_w, w_ffn_up, ffn_conv_w, ffn_conv_b,
                  w_ffn_down, final_norm_w):
    o_z = 0
    o_xbc = o_z + SSD_D_INNER
    o_dt = o_xbc + SSD_XBC
    o_q = o_dt + 2 * SSD_N_HEADS
    o_k = o_q + GLA_DK
    o_v = o_k + GLA_DK
    o_g = o_v + GLA_DV
    o_gk = o_g + GLA_DV
    o_gate = o_gk + 2 * GLA_GATE_RANK
    w_big = jnp.concatenate([w_in[:, o_xbc:o_dt], w_in[:, o_q:o_gk], w_in[:, o_z:o_xbc], w_in[:, o_gate:]],
                            axis=1).astype(BF16)
    pad = SMALL_W - 2 * SSD_N_HEADS - 2 * GLA_GATE_RANK
    w_small = jnp.concatenate([w_in[:, o_dt:o_q], w_in[:, o_gk:o_gate], jnp.zeros((D_MODEL, pad), F32)],
                              axis=1).astype(BF16)

    def row128(v):
        return jnp.concatenate([v.reshape(-1), jnp.zeros((SMALL_W - v.size,), F32)]).reshape(1, SMALL_W)

    head_of_lane = np.arange(SSD_D_INNER) // SSD_HEAD_DIM
    ef = (np.arange(SMALL_W)[:, None] == head_of_lane[None, :])
    eb = (np.arange(SMALL_W)[:, None] == (head_of_lane[None, :] + SSD_N_HEADS))
    w2 = jnp.zeros((SMALL_W, 2 * GLA_DK), F32)
    w2 = w2.at[GK_OFF:GK_OFF + GLA_GATE_RANK, :GLA_DK].set(gla_gate_w2[0])
    w2 = w2.at[GK_OFF + GLA_GATE_RANK:GK_OFF + 2 * GLA_GATE_RANK, GLA_DK:].set(gla_gate_w2[1])
    return {
        "norm1_w": norm1_w.reshape(1, D_MODEL),
        "w_big": w_big,
        "w_small": w_small,
        "ssd_conv_w": ssd_conv_w,
        "ssd_conv_b": ssd_conv_b.reshape(1, SSD_XBC),
        "ssd_bias": row128(ssd_dt_bias),
        "ssd_nega": row128(-jnp.exp(ssd_a_log)),
        "ssd_tri": _block_tri(SSD_TILE, SSD_Q, "lower"),
        "ssd_ones": _block_tri(SSD_TILE, SSD_Q, "ones"),
        "ssd_tri_full": _block_tri(SSD_TILE, SSD_TILE, "lower"),
        "ssd_ones_full": _block_tri(SSD_TILE, SSD_TILE, "ones"),
        "ssd_ef2": jnp.asarray(np.concatenate([ef, ef], axis=0), dtype=BF16),
        "ssd_eb2": jnp.asarray(np.concatenate([eb, eb], axis=0), dtype=BF16),
        "ssd_drow": jnp.repeat(ssd_d, SSD_HEAD_DIM).reshape(1, SSD_D_INNER),
        "ssd_normw": ssd_norm_w.reshape(1, SSD_D_INNER),
        "w_ssd_out": w_ssd_out.astype(BF16),
        "gla_w2": w2.astype(BF16),
        "gla_gb": gla_gate_b.reshape(1, 2 * GLA_DK),
        "gla_tril": _block_tri(GLA_TILE, GLA_CHUNK, "lower"),
        "gla_triu": _block_tri(GLA_TILE, GLA_CHUNK, "upper"),
        "gla_triu_full": _block_tri(GLA_TILE, GLA_TILE, "upper"),
        "gla_normw": gla_norm_w.reshape(1, GLA_DV),
        "w_gla_out": w_gla_out.astype(BF16),
        "w_o": w_o.astype(BF16),
        "norm2_w": norm2_w.reshape(1, D_MODEL),
        "w_ffn_up": w_ffn_up.astype(BF16),
        "ffn_conv_w": ffn_conv_w,
        "ffn_conv_b": ffn_conv_b.reshape(1, 2 * D_FF),
        "w_ffn_down": w_ffn_down.astype(BF16),
        "final_norm_w": final_norm_w.reshape(1, D_MODEL),
    }


def _trunk(x3, wts):
    batch, seq_len, _ = x3.shape
    x = x3.reshape(batch * seq_len, D_MODEL)
    big, small = _norm_proj(x, wts["norm1_w"], wts["w_big"], wts["w_small"])
    a_gated = _ssd(big, small, batch, seq_len, wts)
    x1 = _gla(big, small, x, a_gated, batch, seq_len, wts)
    out = _ffn(x1, batch, seq_len, wts)
    return out.reshape(batch, seq_len, D_MODEL)


def kernel(x_prompt, x_sample, norm1_w, w_in, ssd_conv_w, ssd_conv_b, ssd_a_log, ssd_dt_bias, ssd_d, ssd_norm_w,
           w_ssd_out, gla_gate_w2, gla_gate_b, gla_norm_w, w_gla_out, w_o, norm2_w, w_ffn_up, ffn_conv_w,
           ffn_conv_b, w_ffn_down, final_norm_w):
    wts = _prep_weights(norm1_w[0], w_in[0], ssd_conv_w[0], ssd_conv_b[0], ssd_a_log[0], ssd_dt_bias[0], ssd_d[0],
                        ssd_norm_w[0], w_ssd_out[0], gla_gate_w2[0], gla_gate_b[0], gla_norm_w[0], w_gla_out[0],
                        w_o[0], norm2_w[0], w_ffn_up[0], ffn_conv_w[0], ffn_conv_b[0], w_ffn_down[0], final_norm_w)
    return (_trunk(x_prompt, wts), _trunk(x_sample, wts))
```

```python
import functools

import numpy as np
import jax
import jax.numpy as jnp
from jax import lax
from jax.experimental import pallas as pl
from jax.experimental.pallas import tpu as pltpu

F32 = jnp.float32
BF16 = jnp.bfloat16

D_MODEL = 1024
SSD_D_INNER = 2048
SSD_HEAD_DIM = 64
SSD_N_HEADS = 32
SSD_N_GROUPS = 4
SSD_D_STATE = 128
SSD_GROUP_W = SSD_D_INNER // SSD_N_GROUPS
SSD_BC_W = SSD_N_GROUPS * SSD_D_STATE
SSD_XBC = SSD_D_INNER + 2 * SSD_BC_W
SSD_CONV = 5
GLA_N_HEADS = 4
GLA_DK = 512
GLA_DV = 1024
GLA_HEAD_K = 128
GLA_HEAD_V = 256
GLA_GATE_RANK = 16
GLA_GATE_NORM = 16.0
GLA_CHUNK = 64
D_FF = 2816
EPS = 1e-6

LANES = 128
SUBLANES = 8
VMEM_LIMIT = 56 * 1024 * 1024

BIG_W = SSD_XBC + 2 * GLA_DK + 2 * GLA_DV + SSD_D_INNER + 2 * D_MODEL
SMALL_W = LANES
GK_OFF = 2 * SSD_N_HEADS

SSD_TILE = 256
SSD_Q = 64
GLA_TILE = 256
HALO = SUBLANES
SSD_HALO = 2 * SUBLANES
CONV_SLAB = 256
FFN_TILE = 512
FFN_F = 1408
PROJ_TM = 1024
PROJ_TN = 1024


def _dot(a, b):
    return jnp.dot(a, b, preferred_element_type=F32)


def _dot_tn(a, b):
    return lax.dot_general(a, b, (((0,), (0,)), ((), ())), preferred_element_type=F32)


def _dot_nt(a, b):
    return lax.dot_general(a, b, (((1,), (1,)), ((), ())), preferred_element_type=F32)


def _split_bf16(x, n):
    parts = []
    r = x
    for i in range(n):
        p = r.astype(BF16)
        parts.append(p)
        if i + 1 < n:
            r = r - p.astype(F32)
    return parts


def _dot_exact_lhs(m_bf16, x, n=3):
    acc = None
    for p in _split_bf16(x, n):
        t = _dot(m_bf16, p)
        acc = t if acc is None else acc + t
    return acc


def _sigmoid(x):
    return 0.5 * jnp.tanh(0.5 * x) + 0.5


def _silu(x):
    return x * _sigmoid(x)


def _softplus(x):
    return jnp.maximum(x, 0.0) + jnp.log1p(jnp.exp(-jnp.abs(x)))


def _cparams(n_axes):
    return pltpu.CompilerParams(dimension_semantics=("arbitrary",) * n_axes,
                                vmem_limit_bytes=VMEM_LIMIT)


def _proj_kernel(x_ref, nw_ref, w_ref, ws_ref, o_ref, os_ref, h_ref):
    @pl.when(pl.program_id(1) == 0)
    def _():
        x = x_ref[...]
        ms = jnp.mean(x * x, axis=-1, keepdims=True)
        h = (x * lax.rsqrt(ms + EPS) * nw_ref[...]).astype(BF16)
        h_ref[...] = h
        os_ref[...] = _dot(h, ws_ref[...])

    o_ref[...] = _dot(h_ref[...], w_ref[...]).astype(o_ref.dtype)


def _norm_proj(x, nw, w_big, w_small):
    nt, d = x.shape
    n = w_big.shape[1]
    tm = min(PROJ_TM, nt)
    tn = PROJ_TN
    return pl.pallas_call(
        _proj_kernel,
        grid=(nt // tm, n // tn),
        in_specs=[pl.BlockSpec((tm, d), lambda i, j: (i, 0)),
                  pl.BlockSpec((1, d), lambda i, j: (0, 0)),
                  pl.BlockSpec((d, tn), lambda i, j: (0, j)),
                  pl.BlockSpec((d, SMALL_W), lambda i, j: (0, 0))],
        out_specs=[pl.BlockSpec((tm, tn), lambda i, j: (i, j)),
                   pl.BlockSpec((tm, SMALL_W), lambda i, j: (i, 0))],
        out_shape=[jax.ShapeDtypeStruct((nt, n), BF16),
                   jax.ShapeDtypeStruct((nt, SMALL_W), F32)],
        scratch_shapes=[pltpu.VMEM((tm, d), BF16)],
        compiler_params=_cparams(2),
        name="norm_proj",
    )(x, nw, w_big, w_small)


def _conv_taps(ext_ref, w_ref, b_ref, row0, n_rows, lo, hi):
    pad = (SSD_CONV - 1) // 2
    acc = b_ref[:, lo:hi]
    for k in range(SSD_CONV):
        acc = acc + ext_ref[row0 - pad + k:row0 - pad + k + n_rows, lo:hi] * w_ref[k:k + 1, lo:hi]
    return acc


def _ssd_conv_silu(prev_ref, cur_ref, next_ref, w_ref, b_ref, shift_ref, ext_ref, out_ref, first, last):
    tile = SSD_TILE
    halo = SSD_HALO
    centre = (SSD_CONV - 1) // 2
    side_taps = [k for k in range(SSD_CONV) if k != centre]
    for lo in range(0, SSD_XBC, CONV_SLAB):
        hi = lo + CONV_SLAB
        x = cur_ref[:, lo:hi]
        acc = x.astype(F32) * w_ref[centre:centre + 1, lo:hi] + b_ref[:, lo:hi]
        for i, k in enumerate(side_taps):
            acc = acc + _dot(shift_ref[i], x) * w_ref[k:k + 1, lo:hi]
        out_ref[:, lo:hi] = _silu(acc).astype(BF16)

    ext_ref[0, 0:halo, :] = jnp.where(first, 0.0, prev_ref[...].astype(F32))
    ext_ref[0, halo:3 * halo, :] = cur_ref[0:2 * halo, :].astype(F32)
    ext_ref[1, 0:2 * halo, :] = cur_ref[tile - 2 * halo:tile, :].astype(F32)
    ext_ref[1, 2 * halo:3 * halo, :] = jnp.where(last, 0.0, next_ref[...].astype(F32))
    for lo in range(0, SSD_XBC, CONV_SLAB):
        hi = lo + CONV_SLAB
        top = _conv_taps(ext_ref.at[0], w_ref, b_ref, halo, halo, lo, hi)
        out_ref[0:halo, lo:hi] = _silu(top).astype(BF16)
        bot = _conv_taps(ext_ref.at[1], w_ref, b_ref, halo, halo, lo, hi)
        out_ref[tile - halo:tile, lo:hi] = _silu(bot).astype(BF16)


def _ssd_small(small_ref, bias_ref, nega_ref, tri_ref, ones_ref):
    s = small_ref[...]
    dt = _softplus(s + bias_ref[...])
    a = dt * nega_ref[...]
    p = _dot_exact_lhs(tri_ref[...], a)
    tot = _dot_exact_lhs(ones_ref[...], a)
    col = lax.broadcasted_iota(jnp.int32, s.shape, 1)
    fwd = col < SSD_N_HEADS
    r = tot - p + a
    pr = jnp.where(fwd, p, r)
    w = dt * jnp.exp(jnp.where(fwd, tot - p, p - a))
    rowarg = jnp.log(dt) - pr
    dsum = pltpu.roll(dt, 2 * SSD_N_HEADS, 1) + pltpu.roll(dt, SSD_N_HEADS, 1)
    diag_cols = (col >= 2 * SSD_N_HEADS) & (col < 3 * SSD_N_HEADS)
    rowarg = jnp.where(diag_cols, jnp.log(dsum), rowarg)
    return pr, w, rowarg


def _expand2(x, e2_ref):
    hi, lo = _split_bf16(x, 2)
    return _dot(jnp.concatenate([hi, lo], axis=1), e2_ref[...])


def _ssd_bwd_state_step(sb, xs_c, b_c, wb_c, decb_row):
    out = []
    for g in range(SSD_N_GROUPS):
        gl = slice(g * SSD_GROUP_W, (g + 1) * SSD_GROUP_W)
        xd = xs_c[:, gl] * wb_c[:, gl]
        bg = b_c[:, g * SSD_D_STATE:(g + 1) * SSD_D_STATE]
        out.append(decb_row[:, gl] * sb[g] + _dot_tn(bg, xd))
    return out


def _ssd_pre_kernel(prev_ref, cur_ref, next_ref, small_ref, convw_ref, convb_ref, shift_ref, bias_ref, nega_ref,
                    tri_ref, ones_ref, eb2_ref, sbin_ref, xconv_ref,
                    ext_ref, sb_ref, *, n_tiles):
    t = pl.program_id(1)
    tile = SSD_TILE

    @pl.when(t == 0)
    def _():
        sb_ref[...] = jnp.zeros_like(sb_ref)

    sbin_ref[0, 0] = sb_ref[...]

    _ssd_conv_silu(prev_ref, cur_ref, next_ref, convw_ref, convb_ref, shift_ref, ext_ref, xconv_ref,
                   t == n_tiles - 1, t == 0)

    pr, w, _ = _ssd_small(small_ref, bias_ref, nega_ref, tri_ref, ones_ref)
    wb = _dot(w.astype(BF16), eb2_ref[0:LANES, :]).astype(BF16)
    decb = _expand2(jnp.exp(pr[0:SUBLANES, :]), eb2_ref)[0:1, :]
    sb = [sb_ref[g] for g in range(SSD_N_GROUPS)]
    new = _ssd_bwd_state_step(sb, xconv_ref[:, 0:SSD_D_INNER],
                              xconv_ref[:, SSD_D_INNER:SSD_D_INNER + SSD_BC_W], wb, decb)
    for g in range(SSD_N_GROUPS):
        sb_ref[g] = new[g]


def _ssd_main_kernel(xc_ref, z_ref, ga_ref, small_ref, sbin_ref,
                     bias_ref, nega_ref, tri_ref, ones_ref, ef2_ref, eb2_ref,
                     drow_ref, normw_ref, wout_ref, o_ref,
                     rowarg_ref, prf_ref, prb_ref, wf_ref, wb_ref, exf_ref, exb_ref,
                     sf_ref, sbc_ref, y_ref):
    t = pl.program_id(1)
    tile = SSD_TILE
    q = SSD_Q
    nc = tile // q
    b_off = SSD_D_INNER
    c_off = SSD_D_INNER + SSD_BC_W

    @pl.when(t == 0)
    def _():
        sf_ref[...] = jnp.zeros_like(sf_ref)

    pr, w, rowarg = _ssd_small(small_ref, bias_ref, nega_ref, tri_ref, ones_ref)
    rowarg_ref[...] = rowarg
    wbf = w.astype(BF16)
    wf_ref[...] = _dot(wbf, ef2_ref[0:LANES, :]).astype(BF16)
    wb_ref[...] = _dot(wbf, eb2_ref[0:LANES, :]).astype(BF16)
    prf = _expand2(pr, ef2_ref)
    prf_ref[...] = prf
    exf_ref[...] = jnp.exp(prf)
    prb = _expand2(pr, eb2_ref)
    prb_ref[...] = prb
    exb_ref[...] = jnp.exp(prb)

    sbc_ref[nc - 1] = sbin_ref[0, 0]

    for c in range(nc - 1, 0, -1):
        r0 = c * q
        rows = slice(r0, r0 + q)
        sb = [sbc_ref[c, g] for g in range(SSD_N_GROUPS)]
        new = _ssd_bwd_state_step(sb, xc_ref[rows, 0:SSD_D_INNER], xc_ref[rows, b_off:c_off],
                                  wb_ref[rows, :], exb_ref[r0:r0 + 1, :])
        for g in range(SSD_N_GROUPS):
            sbc_ref[c - 1, g] = new[g]

    ii = lax.broadcasted_iota(jnp.int32, (q, LANES), 0)
    ll = lax.broadcasted_iota(jnp.int32, (q, LANES), 1)
    jj = jnp.where(ll < SSD_HEAD_DIM, ll, ll - SSD_HEAD_DIM)
    lower = ii > jj
    upper = ii < jj
    left = ll < SSD_HEAD_DIM
    left_row = lax.broadcasted_iota(jnp.int32, (1, LANES), 1) < SSD_HEAD_DIM
    hpg = SSD_N_HEADS // SSD_N_GROUPS

    for k in range(nc // 2):
        k0 = k * 2 * q
        row_t = rowarg_ref[k0:k0 + 2 * q, :].T
        row_tr = pltpu.roll(row_t, q, 1)
        for s in range(2):
            r0 = k0 + s * q
            rows = slice(r0, r0 + q)
            c = 2 * k + s
            lo_src, hi_src = (row_t, row_tr) if s == 0 else (row_tr, row_t)

            def pair_row(base, h0):
                return jnp.where(left_row, lo_src[base + h0:base + h0 + 1, :],
                                 hi_src[base + h0 + 1:base + h0 + 2, :])

            for g in range(SSD_N_GROUPS):
                gl = slice(g * SSD_GROUP_W, (g + 1) * SSD_GROUP_W)
                b_g = xc_ref[rows, b_off + g * SSD_D_STATE:b_off + (g + 1) * SSD_D_STATE]
                c_g = xc_ref[rows, c_off + g * SSD_D_STATE:c_off + (g + 1) * SSD_D_STATE]
                cb2 = _dot_nt(c_g, jnp.concatenate([b_g, b_g], axis=0))
                sf = sf_ref[g]
                y_off = (_dot(c_g, sf.astype(BF16)) * exf_ref[rows, gl]
                         + _dot(c_g, sbc_ref[c, g].astype(BF16)) * exb_ref[rows, gl])
                xs_g = xc_ref[rows, gl].astype(F32)
                for pm in range(hpg // 2):
                    h0 = g * hpg + 2 * pm
                    cl = slice(g * SSD_GROUP_W + pm * LANES, g * SSD_GROUP_W + (pm + 1) * LANES)
                    arg = jnp.where(
                        lower, prf_ref[rows, cl] + pair_row(0, h0),
                        jnp.where(upper, prb_ref[rows, cl] + pair_row(SSD_N_HEADS, h0),
                                  pair_row(2 * SSD_N_HEADS, h0)))
                    m2 = (cb2 * jnp.exp(arg)).astype(BF16)
                    x_pair = xs_g[:, pm * LANES:(pm + 1) * LANES]
                    x2 = jnp.concatenate([jnp.where(left, x_pair, 0.0), jnp.where(left, 0.0, x_pair)],
                                         axis=0).astype(BF16)
                    y_ref[rows, cl] = (_dot(m2, x2) + y_off[:, pm * LANES:(pm + 1) * LANES]
                                       + drow_ref[:, cl] * x_pair)
                xd = xc_ref[rows, gl] * wf_ref[rows, gl]
                decf = exf_ref[r0 + q - 1:r0 + q, gl]
                sf_ref[g] = decf * sf + _dot_tn(b_g, xd)

    z = z_ref[...].astype(F32)
    y = y_ref[...] * _silu(z)
    outs = []
    for g in range(SSD_N_GROUPS):
        gl = slice(g * SSD_GROUP_W, (g + 1) * SSD_GROUP_W)
        yg = y[:, gl]
        ms = jnp.mean(yg * yg, axis=-1, keepdims=True)
        outs.append((yg * lax.rsqrt(ms + EPS) * normw_ref[:, gl]).astype(BF16))
    yn = jnp.concatenate(outs, axis=1)
    o_ref[...] = (_sigmoid(ga_ref[...].astype(F32)) * _dot(yn, wout_ref[...])).astype(o_ref.dtype)


def _const_spec(arr):
    nd = arr.ndim
    return pl.BlockSpec(arr.shape, lambda b, t: (0,) * nd)


def _ssd(big, small, batch, seq_len, wts):
    nt = batch * seq_len
    tile = SSD_TILE
    n_tiles = seq_len // tile
    nc = tile // SSD_Q
    rb = tile // SSD_HALO
    seq_rb = seq_len // SSD_HALO
    last_rb = nt // SSD_HALO - 1
    state_shape = (SSD_N_GROUPS, SSD_D_STATE, SSD_GROUP_W)
    rev = lambda t: n_tiles - 1 - t
    consts_pre = [wts["ssd_conv_w"], wts["ssd_conv_b"], wts["ssd_shift"], wts["ssd_bias"], wts["ssd_nega"],
                  wts["ssd_tri_full"], wts["ssd_ones_full"], wts["ssd_eb2"]]
    sb_in, xconv = pl.pallas_call(
        functools.partial(_ssd_pre_kernel, n_tiles=n_tiles),
        grid=(batch, n_tiles),
        in_specs=[pl.BlockSpec((SSD_HALO, SSD_XBC),
                               lambda b, t: (jnp.maximum(b * seq_rb + rev(t) * rb - 1, 0), 0)),
                  pl.BlockSpec((tile, SSD_XBC), lambda b, t: (b * n_tiles + rev(t), 0)),
                  pl.BlockSpec((SSD_HALO, SSD_XBC),
                               lambda b, t: (jnp.minimum(b * seq_rb + (rev(t) + 1) * rb, last_rb), 0)),
                  pl.BlockSpec((tile, SMALL_W), lambda b, t: (b * n_tiles + rev(t), 0))]
        + [_const_spec(c) for c in consts_pre],
        out_specs=[pl.BlockSpec((1, 1) + state_shape, lambda b, t: (b, rev(t), 0, 0, 0)),
                   pl.BlockSpec((tile, SSD_XBC), lambda b, t: (b * n_tiles + rev(t), 0))],
        out_shape=[jax.ShapeDtypeStruct((batch, n_tiles) + state_shape, F32),
                   jax.ShapeDtypeStruct((nt, SSD_XBC), BF16)],
        scratch_shapes=[pltpu.VMEM((2, 3 * SSD_HALO, SSD_XBC), F32),
                        pltpu.VMEM(state_shape, F32)],
        compiler_params=_cparams(2),
        name="ssd_pre",
    )(big, big, big, small, *consts_pre)

    consts = [wts["ssd_bias"], wts["ssd_nega"], wts["ssd_tri"], wts["ssd_ones"], wts["ssd_ef2"], wts["ssd_eb2"],
              wts["ssd_drow"], wts["ssd_normw"], wts["w_ssd_out"]]
    z_block = (SSD_XBC + 2 * GLA_DK + 2 * GLA_DV) // SSD_D_INNER
    gate_a_block = (BIG_W - 2 * D_MODEL) // D_MODEL
    wide = pltpu.VMEM((tile, SSD_D_INNER), F32)
    wide16 = pltpu.VMEM((tile, SSD_D_INNER), BF16)
    return pl.pallas_call(
        _ssd_main_kernel,
        grid=(batch, n_tiles),
        in_specs=[pl.BlockSpec((tile, SSD_XBC), lambda b, t: (b * n_tiles + t, 0)),
                  pl.BlockSpec((tile, SSD_D_INNER), lambda b, t: (b * n_tiles + t, z_block)),
                  pl.BlockSpec((tile, D_MODEL), lambda b, t: (b * n_tiles + t, gate_a_block)),
                  pl.BlockSpec((tile, SMALL_W), lambda b, t: (b * n_tiles + t, 0)),
                  pl.BlockSpec((1, 1) + state_shape, lambda b, t: (b, t, 0, 0, 0))]
        + [_const_spec(c) for c in consts],
        out_specs=pl.BlockSpec((tile, D_MODEL), lambda b, t: (b * n_tiles + t, 0)),
        out_shape=jax.ShapeDtypeStruct((nt, D_MODEL), BF16),
        scratch_shapes=[pltpu.VMEM((tile, SMALL_W), F32),
                        wide, wide, wide16, wide16, wide, wide,
                        pltpu.VMEM(state_shape, F32),
                        pltpu.VMEM((nc,) + state_shape, F32),
                        wide],
        compiler_params=_cparams(2),
        name="ssd_main",
    )(xconv, big, big, small, sb_in, *consts)


def _gla_log_alpha(small_ref, w2, gb):
    pre = _dot(small_ref[...].astype(BF16), w2) + gb
    return -(jnp.maximum(-pre, 0.0) + jnp.log(1.0 + jnp.exp(-jnp.abs(pre)))) * (1.0 / GLA_GATE_NORM)


def _gla_bwd_state_step(sb_h, k_h, v_h, cs_h):
    tot = cs_h[0:1, :]
    k_dec = (k_h * jnp.exp(tot - cs_h)).astype(BF16)
    return jnp.exp(tot) * sb_h + _dot_tn(v_h, k_dec)


def _gla_pre_kernel(qkvg_ref, small_ref, w2_ref, gb_ref, triu_ref, sbin_ref, sb_ref):
    t = pl.program_id(1)

    @pl.when(t == 0)
    def _():
        sb_ref[...] = jnp.zeros_like(sb_ref)

    sbin_ref[0, 0] = sb_ref[...]
    la = _gla_log_alpha(small_ref, w2_ref[:, GLA_DK:], gb_ref[:, GLA_DK:])
    cs = _dot_exact_lhs(triu_ref[...], la, 2)
    for h in range(GLA_N_HEADS):
        kl = slice(h * GLA_HEAD_K, (h + 1) * GLA_HEAD_K)
        k_h = qkvg_ref[:, GLA_DK + h * GLA_HEAD_K:GLA_DK + (h + 1) * GLA_HEAD_K].astype(F32)
        v_h = qkvg_ref[:, 2 * GLA_DK + h * GLA_HEAD_V:2 * GLA_DK + (h + 1) * GLA_HEAD_V]
        sb_ref[h] = _gla_bwd_state_step(sb_ref[h], k_h, v_h, cs[:, kl])


def _gla_main_kernel(qkvg_ref, small_ref, sbin_ref, x_ref, a_ref, gateb_ref,
                     w2_ref, gb_ref, tril_ref, triu_ref, normw_ref, wout_ref, wo_ref, o_ref,
                     cs_ref, sf_ref, sbc_ref, oacc_ref):
    t = pl.program_id(1)
    tile = GLA_TILE
    q = GLA_CHUNK
    nc = tile // q
    scale = GLA_HEAD_K ** -0.5

    @pl.when(t == 0)
    def _():
        sf_ref[...] = jnp.zeros_like(sf_ref)

    la = _gla_log_alpha(small_ref, w2_ref[...], gb_ref[...])
    cs_ref[:, 0:GLA_DK] = _dot_exact_lhs(tril_ref[...], la[:, 0:GLA_DK], 2)
    cs_ref[:, GLA_DK:2 * GLA_DK] = _dot_exact_lhs(triu_ref[...], la[:, GLA_DK:2 * GLA_DK], 2)

    sbc_ref[nc - 1] = sbin_ref[0, 0]

    for c in range(nc - 1, 0, -1):
        rows = slice(c * q, (c + 1) * q)
        for h in range(GLA_N_HEADS):
            kl = slice(GLA_DK + h * GLA_HEAD_K, GLA_DK + (h + 1) * GLA_HEAD_K)
            k_h = qkvg_ref[rows, kl].astype(F32)
            v_h = qkvg_ref[rows, 2 * GLA_DK + h * GLA_HEAD_V:2 * GLA_DK + (h + 1) * GLA_HEAD_V]
            sbc_ref[c - 1, h] = _gla_bwd_state_step(sbc_ref[c, h], k_h, v_h, cs_ref[rows, kl])

    ii = lax.broadcasted_iota(jnp.int32, (q, q), 0)
    jj = lax.broadcasted_iota(jnp.int32, (q, q), 1)
    lower_inc = ii >= jj
    upper_inc = ii <= jj

    for c in range(nc):
        rows = slice(c * q, (c + 1) * q)
        for h in range(GLA_N_HEADS):
            fl = slice(h * GLA_HEAD_K, (h + 1) * GLA_HEAD_K)
            bl = slice(GLA_DK + h * GLA_HEAD_K, GLA_DK + (h + 1) * GLA_HEAD_K)
            q_h = qkvg_ref[rows, fl].astype(F32) * scale
            k_h = qkvg_ref[rows, bl].astype(F32)
            v_h = qkvg_ref[rows, 2 * GLA_DK + h * GLA_HEAD_V:2 * GLA_DK + (h + 1) * GLA_HEAD_V]
            csf = cs_ref[rows, fl]
            csb = cs_ref[rows, bl]
            qf = (q_h * jnp.exp(csf)).astype(BF16)
            kf = (k_h * jnp.exp(-csf)).astype(BF16)
            qb = (q_h * jnp.exp(csb)).astype(BF16)
            kb = (k_h * jnp.exp(-csb)).astype(BF16)
            att = (jnp.where(lower_inc, _dot_nt(qf, kf), 0.0)
                   + jnp.where(upper_inc, _dot_nt(qb, kb), 0.0))
            sf = sf_ref[h]
            s_cat = jnp.concatenate([sf, sbc_ref[c, h]], axis=1).astype(BF16)
            o = _dot(att.astype(BF16), v_h) + _dot_nt(jnp.concatenate([qf, qb], axis=1), s_cat)
            oacc_ref[rows, h * GLA_HEAD_V:(h + 1) * GLA_HEAD_V] = o
            tot = csf[q - 1:q, :]
            k_dec = (k_h * jnp.exp(tot - csf)).astype(BF16)
            sf_ref[h] = jnp.exp(tot) * sf + _dot_tn(v_h, k_dec)

    g = qkvg_ref[:, 2 * GLA_DK + GLA_DV:2 * GLA_DK + 2 * GLA_DV].astype(F32)
    o = oacc_ref[...]
    outs = []
    for h in range(GLA_N_HEADS):
        hl = slice(h * GLA_HEAD_V, (h + 1) * GLA_HEAD_V)
        oh = o[:, hl]
        ms = jnp.mean(oh * oh, axis=-1, keepdims=True)
        outs.append((oh * lax.rsqrt(ms + EPS) * normw_ref[:, hl] * _silu(g[:, hl])).astype(BF16))
    branch_b = _dot(jnp.concatenate(outs, axis=1), wout_ref[...])
    mixed = a_ref[...].astype(F32) + _sigmoid(gateb_ref[...].astype(F32)) * branch_b
    o_ref[...] = x_ref[...] + _dot(mixed.astype(BF16), wo_ref[...])


def _gla(big, small, x, a_gated, batch, seq_len, wts):
    nt = batch * seq_len
    tile = GLA_TILE
    n_tiles = seq_len // tile
    nc = tile // GLA_CHUNK
    qkvg_w = 2 * GLA_DK + 2 * GLA_DV
    qkvg_block = SSD_XBC // qkvg_w
    gate_b_block = (BIG_W - D_MODEL) // D_MODEL
    rev = lambda t: n_tiles - 1 - t
    consts_pre = [wts["gla_w2"], wts["gla_gb"], wts["gla_triu_full"]]
    state_shape = (GLA_N_HEADS, GLA_HEAD_V, GLA_HEAD_K)
    sb_in = pl.pallas_call(
        _gla_pre_kernel,
        grid=(batch, n_tiles),
        in_specs=[pl.BlockSpec((tile, qkvg_w), lambda b, t: (b * n_tiles + rev(t), qkvg_block)),
                  pl.BlockSpec((tile, SMALL_W), lambda b, t: (b * n_tiles + rev(t), 0))]
        + [_const_spec(c) for c in consts_pre],
        out_specs=pl.BlockSpec((1, 1) + state_shape, lambda b, t: (b, rev(t), 0, 0, 0)),
        out_shape=jax.ShapeDtypeStruct((batch, n_tiles) + state_shape, F32),
        scratch_shapes=[pltpu.VMEM(state_shape, F32)],
        compiler_params=_cparams(2),
        name="gla_pre",
    )(big, small, *consts_pre)

    consts = [wts["gla_w2"], wts["gla_gb"], wts["gla_tril"], wts["gla_triu"], wts["gla_normw"],
              wts["w_gla_out"], wts["w_o"]]
    row = lambda b, t: (b * n_tiles + t, 0)
    return pl.pallas_call(
        _gla_main_kernel,
        grid=(batch, n_tiles),
        in_specs=[pl.BlockSpec((tile, qkvg_w), lambda b, t: (b * n_tiles + t, qkvg_block)),
                  pl.BlockSpec((tile, SMALL_W), row),
                  pl.BlockSpec((1, 1) + state_shape, lambda b, t: (b, t, 0, 0, 0)),
                  pl.BlockSpec((tile, D_MODEL), row),
                  pl.BlockSpec((tile, D_MODEL), row),
                  pl.BlockSpec((tile, D_MODEL), lambda b, t: (b * n_tiles + t, gate_b_block))]
        + [_const_spec(c) for c in consts],
        out_specs=pl.BlockSpec((tile, D_MODEL), row),
        out_shape=jax.ShapeDtypeStruct((nt, D_MODEL), F32),
        scratch_shapes=[pltpu.VMEM((tile, 2 * GLA_DK), F32),
                        pltpu.VMEM(state_shape, F32),
                        pltpu.VMEM((nc,) + state_shape, F32),
                        pltpu.VMEM((tile, GLA_DV), F32)],
        compiler_params=_cparams(2),
        name="gla_main",
    )(big, small, sb_in, x, a_gated, big, *consts)


def _ffn_kernel(prev_ref, cur_ref, next_ref, n2_ref, wg_ref, wv_ref, cwg_ref, cwv_ref, cbg_ref, cbv_ref,
                wd_ref, fn_ref, o_ref, h_ref, ug_ref, uv_ref, acc_ref, *, n_tiles):
    t = pl.program_id(1)
    f = pl.program_id(2)
    nf = pl.num_programs(2)
    tile = FFN_TILE

    def norm_rows(x):
        ms = jnp.mean(x * x, axis=-1, keepdims=True)
        return x * lax.rsqrt(ms + EPS) * n2_ref[...]

    @pl.when(f == 0)
    def _():
        h_ref[0:HALO, :] = norm_rows(prev_ref[...])
        h_ref[HALO:HALO + tile, :] = norm_rows(cur_ref[...])
        h_ref[HALO + tile:HALO + tile + HALO, :] = norm_rows(next_ref[...])
        acc_ref[...] = jnp.zeros_like(acc_ref)

    h = h_ref[...].astype(BF16)
    row = lax.broadcasted_iota(jnp.int32, (tile + 2 * HALO, 1), 0)
    keep = jnp.logical_and(jnp.logical_or(row >= HALO, t > 0),
                           jnp.logical_or(row < HALO + tile, t < n_tiles - 1))
    ug_ref[...] = jnp.where(keep, _dot(h, wg_ref[...]), 0.0)
    uv_ref[...] = jnp.where(keep, _dot(h, wv_ref[...]), 0.0)

    def conv3(u_ref, cw_ref, cb_ref):
        acc = None
        for k in range(3):
            term = u_ref[HALO - 1 + k:HALO - 1 + k + tile, :] * cw_ref[k:k + 1, :]
            acc = term if acc is None else acc + term
        return acc + cb_ref[...]

    act = _silu(conv3(ug_ref, cwg_ref, cbg_ref)) * conv3(uv_ref, cwv_ref, cbv_ref)
    acc_ref[...] += _dot(act.astype(BF16), wd_ref[...])

    @pl.when(f == nf - 1)
    def _():
        x2 = cur_ref[...] + acc_ref[...]
        ms = jnp.mean(x2 * x2, axis=-1, keepdims=True)
        o_ref[...] = x2 * lax.rsqrt(ms + EPS) * fn_ref[...]


def _ffn(x1, batch, seq_len, wts):
    nt = batch * seq_len
    tile = FFN_TILE
    n_tiles = seq_len // tile
    nf = D_FF // FFN_F
    rb = tile // HALO
    seq_rb = seq_len // HALO
    last_rb = nt // HALO - 1
    c2 = lambda b, t, f: (0, 0)
    return pl.pallas_call(
        functools.partial(_ffn_kernel, n_tiles=n_tiles),
        grid=(batch, n_tiles, nf),
        in_specs=[pl.BlockSpec((HALO, D_MODEL), lambda b, t, f: (jnp.maximum(b * seq_rb + t * rb - 1, 0), 0)),
                  pl.BlockSpec((tile, D_MODEL), lambda b, t, f: (b * n_tiles + t, 0)),
                  pl.BlockSpec((HALO, D_MODEL),
                               lambda b, t, f: (jnp.minimum(b * seq_rb + (t + 1) * rb, last_rb), 0)),
                  pl.BlockSpec((1, D_MODEL), c2),
                  pl.BlockSpec((D_MODEL, FFN_F), lambda b, t, f: (0, f)),
                  pl.BlockSpec((D_MODEL, FFN_F), lambda b, t, f: (0, nf + f)),
                  pl.BlockSpec((3, FFN_F), lambda b, t, f: (0, f)),
                  pl.BlockSpec((3, FFN_F), lambda b, t, f: (0, nf + f)),
                  pl.BlockSpec((1, FFN_F), lambda b, t, f: (0, f)),
                  pl.BlockSpec((1, FFN_F), lambda b, t, f: (0, nf + f)),
                  pl.BlockSpec((FFN_F, D_MODEL), lambda b, t, f: (f, 0)),
                  pl.BlockSpec((1, D_MODEL), c2)],
        out_specs=pl.BlockSpec((tile, D_MODEL), lambda b, t, f: (b * n_tiles + t, 0)),
        out_shape=jax.ShapeDtypeStruct((nt, D_MODEL), F32),
        scratch_shapes=[pltpu.VMEM((tile + 2 * HALO, D_MODEL), F32),
                        pltpu.VMEM((tile + 2 * HALO, FFN_F), F32),
                        pltpu.VMEM((tile + 2 * HALO, FFN_F), F32),
                        pltpu.VMEM((tile, D_MODEL), F32)],
        compiler_params=_cparams(3),
        name="ffn",
    )(x1, x1, x1, wts["norm2_w"], wts["w_ffn_up"], wts["w_ffn_up"], wts["ffn_conv_w"], wts["ffn_conv_w"],
      wts["ffn_conv_b"], wts["ffn_conv_b"], wts["w_ffn_down"], wts["final_norm_w"])


def _block_tri(n, blk, kind):
    i = np.arange(n)[:, None]
    j = np.arange(n)[None, :]
    m = (i // blk) == (j // blk)
    if kind == "lower":
        m = m & (i >= j)
    elif kind == "upper":
        m = m & (i <= j)
    return jnp.asarray(m, dtype=BF16)


def _shift_mats(n, n_taps):
    pad = (n_taps - 1) // 2
    i = np.arange(n)[:, None]
    j = np.arange(n)[None, :]
    mats = [(j == i + k - pad) for k in range(n_taps) if k != pad]
    return jnp.asarray(np.stack(mats), dtype=BF16)


def _prep_weights(norm1_w, w_in, ssd_conv_w, ssd_conv_b, ssd_a_log, ssd_dt_bias, ssd_d, ssd_norm_w, w_ssd_out,
                  gla_gate_w2, gla_gate_b, gla_norm_w, w_gla_out, w_o, norm2_w, w_ffn_up, ffn_conv_w, ffn_conv_b,
                  w_ffn_down, final_norm_w):
    o_z = 0
    o_xbc = o_z + SSD_D_INNER
    o_dt = o_xbc + SSD_XBC
    o_q = o_dt + 2 * SSD_N_HEADS
    o_k = o_q + GLA_DK
    o_v = o_k + GLA_DK
    o_g = o_v + GLA_DV
    o_gk = o_g + GLA_DV
    o_gate = o_gk + 2 * GLA_GATE_RANK
    w_big = jnp.concatenate([w_in[:, o_xbc:o_dt], w_in[:, o_q:o_gk], w_in[:, o_z:o_xbc], w_in[:, o_gate:]],
                            axis=1).astype(BF16)
    pad = SMALL_W - 2 * SSD_N_HEADS - 2 * GLA_GATE_RANK
    w_small = jnp.concatenate([w_in[:, o_dt:o_q], w_in[:, o_gk:o_gate], jnp.zeros((D_MODEL, pad), F32)],
                              axis=1).astype(BF16)

    def row128(v):
        return jnp.concatenate([v.reshape(-1), jnp.zeros((SMALL_W - v.size,), F32)]).reshape(1, SMALL_W)

    head_of_lane = np.arange(SSD_D_INNER) // SSD_HEAD_DIM
    ef = (np.arange(SMALL_W)[:, None] == head_of_lane[None, :])
    eb = (np.arange(SMALL_W)[:, None] == (head_of_lane[None, :] + SSD_N_HEADS))
    w2 = jnp.zeros((SMALL_W, 2 * GLA_DK), F32)
    w2 = w2.at[GK_OFF:GK_OFF + GLA_GATE_RANK, :GLA_DK].set(gla_gate_w2[0])
    w2 = w2.at[GK_OFF + GLA_GATE_RANK:GK_OFF + 2 * GLA_GATE_RANK, GLA_DK:].set(gla_gate_w2[1])
    return {
        "norm1_w": norm1_w.reshape(1, D_MODEL),
        "w_big": w_big,
        "w_small": w_small,
        "ssd_conv_w": ssd_conv_w,
        "ssd_conv_b": ssd_conv_b.reshape(1, SSD_XBC),
        "ssd_shift": _shift_mats(SSD_TILE, SSD_CONV),
        "ssd_bias": row128(ssd_dt_bias),
        "ssd_nega": row128(-jnp.exp(ssd_a_log)),
        "ssd_tri": _block_tri(SSD_TILE, SSD_Q, "lower"),
        "ssd_ones": _block_tri(SSD_TILE, SSD_Q, "ones"),
        "ssd_tri_full": _block_tri(SSD_TILE, SSD_TILE, "lower"),
        "ssd_ones_full": _block_tri(SSD_TILE, SSD_TILE, "ones"),
        "ssd_ef2": jnp.asarray(np.concatenate([ef, ef], axis=0), dtype=BF16),
        "ssd_eb2": jnp.asarray(np.concatenate([eb, eb], axis=0), dtype=BF16),
        "ssd_drow": jnp.repeat(ssd_d, SSD_HEAD_DIM).reshape(1, SSD_D_INNER),
        "ssd_normw": ssd_norm_w.reshape(1, SSD_D_INNER),
        "w_ssd_out": w_ssd_out.astype(BF16),
        "gla_w2": w2.astype(BF16),
        "gla_gb": gla_gate_b.reshape(1, 2 * GLA_DK),
        "gla_tril": _block_tri(GLA_TILE, GLA_CHUNK, "lower"),
        "gla_triu": _block_tri(GLA_TILE, GLA_CHUNK, "upper"),
        "gla_triu_full": _block_tri(GLA_TILE, GLA_TILE, "upper"),
        "gla_normw": gla_norm_w.reshape(1, GLA_DV),
        "w_gla_out": w_gla_out.astype(BF16),
        "w_o": w_o.astype(BF16),
        "norm2_w": norm2_w.reshape(1, D_MODEL),
        "w_ffn_up": w_ffn_up.astype(BF16),
        "ffn_conv_w": ffn_conv_w,
        "ffn_conv_b": ffn_conv_b.reshape(1, 2 * D_FF),
        "w_ffn_down": w_ffn_down.astype(BF16),
        "final_norm_w": final_norm_w.reshape(1, D_MODEL),
    }


def _trunk(x3, wts):
    batch, seq_len, _ = x3.shape
    x = x3.reshape(batch * seq_len, D_MODEL)
    big, small = _norm_proj(x, wts["norm1_w"], wts["w_big"], wts["w_small"])
    a_gated = _ssd(big, small, batch, seq_len, wts)
    x1 = _gla(big, small, x, a_gated, batch, seq_len, wts)
    out = _ffn(x1, batch, seq_len, wts)
    return out.reshape(batch, seq_len, D_MODEL)


def kernel(x_prompt, x_sample, norm1_w, w_in, ssd_conv_w, ssd_conv_b, ssd_a_log, ssd_dt_bias, ssd_d, ssd_norm_w,
           w_ssd_out, gla_gate_w2, gla_gate_b, gla_norm_w, w_gla_out, w_o, norm2_w, w_ffn_up, ffn_conv_w,
           ffn_conv_b, w_ffn_down, final_norm_w):
    wts = _prep_weights(norm1_w[0], w_in[0], ssd_conv_w[0], ssd_conv_b[0], ssd_a_log[0], ssd_dt_bias[0], ssd_d[0],
                        ssd_norm_w[0], w_ssd_out[0], gla_gate_w2[0], gla_gate_b[0], gla_norm_w[0], w_gla_out[0],
                        w_o[0], norm2_w[0], w_ffn_up[0], ffn_conv_w[0], ffn_conv_b[0], w_ffn_down[0], final_norm_w)
    return (_trunk(x_prompt, wts), _trunk(x_sample, wts))
```

```python
import functools

import numpy as np
import jax
import jax.numpy as jnp
from jax import lax
from jax.experimental import pallas as pl
from jax.experimental.pallas import tpu as pltpu

F32 = jnp.float32
BF16 = jnp.bfloat16

D_MODEL = 1024
SSD_D_INNER = 2048
SSD_HEAD_DIM = 64
SSD_N_HEADS = 32
SSD_N_GROUPS = 4
SSD_D_STATE = 128
SSD_GROUP_W = SSD_D_INNER // SSD_N_GROUPS
SSD_BC_W = SSD_N_GROUPS * SSD_D_STATE
SSD_XBC = SSD_D_INNER + 2 * SSD_BC_W
SSD_CONV = 5
GLA_N_HEADS = 4
GLA_DK = 512
GLA_DV = 1024
GLA_HEAD_K = 128
GLA_HEAD_V = 256
GLA_GATE_RANK = 16
GLA_GATE_NORM = 16.0
GLA_CHUNK = 64
D_FF = 2816
EPS = 1e-6

LANES = 128
SUBLANES = 8
VMEM_LIMIT = 56 * 1024 * 1024

BIG_W = SSD_XBC + 2 * GLA_DK + 2 * GLA_DV + SSD_D_INNER + 2 * D_MODEL
SMALL_W = LANES
GK_OFF = 2 * SSD_N_HEADS

PRE_TILE = 256
SSD_TILE = 256
SSD_Q = 64
GLA_TILE = 256
HALO = SUBLANES
SSD_HALO = 2 * SUBLANES
CONV_SLAB = 256
FFN_TILE = 512
FFN_F = 1408
REST_TN = 512


def _dot(a, b):
    return jnp.dot(a, b, preferred_element_type=F32)


def _dot_tn(a, b):
    return lax.dot_general(a, b, (((0,), (0,)), ((), ())), preferred_element_type=F32)


def _dot_nt(a, b):
    return lax.dot_general(a, b, (((1,), (1,)), ((), ())), preferred_element_type=F32)


def _split_bf16(x, n):
    parts = []
    r = x
    for i in range(n):
        p = r.astype(BF16)
        parts.append(p)
        if i + 1 < n:
            r = r - p.astype(F32)
    return parts


def _dot_exact_lhs(m_bf16, x, n=3):
    acc = None
    for p in _split_bf16(x, n):
        t = _dot(m_bf16, p)
        acc = t if acc is None else acc + t
    return acc


def _sigmoid(x):
    return 0.5 * jnp.tanh(0.5 * x) + 0.5


def _silu(x):
    return x * _sigmoid(x)


def _softplus(x):
    return jnp.maximum(x, 0.0) + jnp.log1p(jnp.exp(-jnp.abs(x)))


def _cparams(n_axes):
    return pltpu.CompilerParams(dimension_semantics=("arbitrary",) * n_axes,
                                vmem_limit_bytes=VMEM_LIMIT)


def _proj_pre_kernel(xprev_ref, xcur_ref, xnext_ref, nw_ref, wbig_ref, wsmall_ref,
                     convw_ref, convb_ref, bias_ref, nega_ref, tri_ref, ones_ref, eb2_ref,
                     w2_ref, gb_ref, triu_ref,
                     big_ref, small_ref, ssd_sbin_ref, gla_sbin_ref,
                     h_ref, ext_ref, ssd_sb_ref, gla_sb_ref, *, n_tiles):
    t = pl.program_id(1)
    tile = PRE_TILE
    halo = SSD_HALO
    first = t == n_tiles - 1
    last = t == 0

    @pl.when(t == 0)
    def _():
        ssd_sb_ref[...] = jnp.zeros_like(ssd_sb_ref)
        gla_sb_ref[...] = jnp.zeros_like(gla_sb_ref)

    ssd_sbin_ref[0, 0] = ssd_sb_ref[...]
    gla_sbin_ref[0, 0] = gla_sb_ref[...]

    def norm_rows(x):
        ms = jnp.mean(x * x, axis=-1, keepdims=True)
        return (x * lax.rsqrt(ms + EPS) * nw_ref[...]).astype(BF16)

    h_ref[0:halo, :] = norm_rows(xprev_ref[...])
    h_ref[halo:halo + tile, :] = norm_rows(xcur_ref[...])
    h_ref[halo + tile:halo + tile + halo, :] = norm_rows(xnext_ref[...])

    row = lax.broadcasted_iota(jnp.int32, (tile + 2 * halo, 1), 0)
    keep = jnp.logical_and(jnp.logical_or(row >= halo, jnp.logical_not(first)),
                           jnp.logical_or(row < halo + tile, jnp.logical_not(last)))
    h = h_ref[halo:halo + tile, :]

    def xbc_slab(lo):
        ext_ref[(lo // CONV_SLAB) % 2] = jnp.where(keep, _dot(h_ref[...], wbig_ref[:, lo:lo + CONV_SLAB]), 0.0)

    def conv_slab(lo):
        ext = ext_ref.at[(lo // CONV_SLAB) % 2]
        y = _conv_taps(ext, convw_ref, convb_ref, halo, tile, lo, lo + CONV_SLAB)
        big_ref[:, lo:lo + CONV_SLAB] = _silu(y).astype(BF16)

    def rest_chunk(lo):
        big_ref[:, lo:lo + REST_TN] = _dot(h, wbig_ref[:, lo:lo + REST_TN]).astype(BF16)

    slabs = list(range(0, SSD_XBC, CONV_SLAB))
    rest = list(range(SSD_XBC, BIG_W, REST_TN))
    xbc_slab(slabs[0])
    for i, lo in enumerate(slabs):
        if i + 1 < len(slabs):
            xbc_slab(slabs[i + 1])
        if rest:
            rest_chunk(rest.pop(0))
        conv_slab(lo)
    for lo in rest:
        rest_chunk(lo)
    small_ref[...] = _dot(h, wsmall_ref[...])

    pr, w, _ = _ssd_small(small_ref, bias_ref, nega_ref, tri_ref, ones_ref)
    wb = _dot(w.astype(BF16), eb2_ref[0:LANES, :]).astype(BF16)
    decb = _expand2(jnp.exp(pr[0:SUBLANES, :]), eb2_ref)[0:1, :]
    sb = [ssd_sb_ref[g] for g in range(SSD_N_GROUPS)]
    new = _ssd_bwd_state_step(sb, big_ref[:, 0:SSD_D_INNER],
                              big_ref[:, SSD_D_INNER:SSD_D_INNER + SSD_BC_W], wb, decb)
    for g in range(SSD_N_GROUPS):
        ssd_sb_ref[g] = new[g]

    la = _gla_log_alpha(small_ref, w2_ref[:, GLA_DK:], gb_ref[:, GLA_DK:])
    cs = _dot_exact_lhs(triu_ref[...], la, 2)
    k_off = SSD_XBC + GLA_DK
    v_off = SSD_XBC + 2 * GLA_DK
    for hd in range(GLA_N_HEADS):
        k_h = big_ref[:, k_off + hd * GLA_HEAD_K:k_off + (hd + 1) * GLA_HEAD_K].astype(F32)
        v_h = big_ref[:, v_off + hd * GLA_HEAD_V:v_off + (hd + 1) * GLA_HEAD_V]
        gla_sb_ref[hd] = _gla_bwd_state_step(gla_sb_ref[hd], k_h, v_h,
                                             cs[:, hd * GLA_HEAD_K:(hd + 1) * GLA_HEAD_K])


def _proj_pre(x, batch, seq_len, wts):
    nt = batch * seq_len
    tile = PRE_TILE
    n_tiles = seq_len // tile
    rb = tile // SSD_HALO
    seq_rb = seq_len // SSD_HALO
    last_rb = nt // SSD_HALO - 1
    rev = lambda t: n_tiles - 1 - t
    ssd_state = (SSD_N_GROUPS, SSD_D_STATE, SSD_GROUP_W)
    gla_state = (GLA_N_HEADS, GLA_HEAD_V, GLA_HEAD_K)
    consts = [wts["ssd_conv_w"], wts["ssd_conv_b"], wts["ssd_bias"], wts["ssd_nega"],
              wts["ssd_tri_full"], wts["ssd_ones_full"], wts["ssd_eb2"],
              wts["gla_w2"], wts["gla_gb"], wts["gla_triu_full"]]
    row = lambda b, t: (b * n_tiles + rev(t), 0)
    return pl.pallas_call(
        functools.partial(_proj_pre_kernel, n_tiles=n_tiles),
        grid=(batch, n_tiles),
        in_specs=[pl.BlockSpec((SSD_HALO, D_MODEL),
                               lambda b, t: (jnp.maximum(b * seq_rb + rev(t) * rb - 1, 0), 0)),
                  pl.BlockSpec((tile, D_MODEL), row),
                  pl.BlockSpec((SSD_HALO, D_MODEL),
                               lambda b, t: (jnp.minimum(b * seq_rb + (rev(t) + 1) * rb, last_rb), 0)),
                  pl.BlockSpec((1, D_MODEL), lambda b, t: (0, 0)),
                  pl.BlockSpec((D_MODEL, BIG_W), lambda b, t: (0, 0), pipeline_mode=pl.Buffered(1)),
                  pl.BlockSpec((D_MODEL, SMALL_W), lambda b, t: (0, 0))]
        + [_const_spec(c) for c in consts],
        out_specs=[pl.BlockSpec((tile, BIG_W), row),
                   pl.BlockSpec((tile, SMALL_W), row),
                   pl.BlockSpec((1, 1) + ssd_state, lambda b, t: (b, rev(t), 0, 0, 0)),
                   pl.BlockSpec((1, 1) + gla_state, lambda b, t: (b, rev(t), 0, 0, 0))],
        out_shape=[jax.ShapeDtypeStruct((nt, BIG_W), BF16),
                   jax.ShapeDtypeStruct((nt, SMALL_W), F32),
                   jax.ShapeDtypeStruct((batch, n_tiles) + ssd_state, F32),
                   jax.ShapeDtypeStruct((batch, n_tiles) + gla_state, F32)],
        scratch_shapes=[pltpu.VMEM((tile + 2 * SSD_HALO, D_MODEL), BF16),
                        pltpu.VMEM((2, tile + 2 * SSD_HALO, CONV_SLAB), F32),
                        pltpu.VMEM(ssd_state, F32),
                        pltpu.VMEM(gla_state, F32)],
        compiler_params=_cparams(2),
        name="proj_pre",
    )(x, x, x, wts["norm1_w"], wts["w_big"], wts["w_small"], *consts)


def _conv_taps(ext_ref, w_ref, b_ref, row0, n_rows, lo, hi):
    pad = (SSD_CONV - 1) // 2
    acc = b_ref[:, lo:hi]
    for k in range(SSD_CONV):
        acc = acc + ext_ref[row0 - pad + k:row0 - pad + k + n_rows, :] * w_ref[k:k + 1, lo:hi]
    return acc


def _ssd_small(small_ref, bias_ref, nega_ref, tri_ref, ones_ref):
    s = small_ref[...]
    dt = _softplus(s + bias_ref[...])
    a = dt * nega_ref[...]
    p = _dot_exact_lhs(tri_ref[...], a)
    tot = _dot_exact_lhs(ones_ref[...], a)
    col = lax.broadcasted_iota(jnp.int32, s.shape, 1)
    fwd = col < SSD_N_HEADS
    r = tot - p + a
    pr = jnp.where(fwd, p, r)
    w = dt * jnp.exp(jnp.where(fwd, tot - p, p - a))
    rowarg = jnp.log(dt) - pr
    dsum = pltpu.roll(dt, 2 * SSD_N_HEADS, 1) + pltpu.roll(dt, SSD_N_HEADS, 1)
    diag_cols = (col >= 2 * SSD_N_HEADS) & (col < 3 * SSD_N_HEADS)
    rowarg = jnp.where(diag_cols, jnp.log(dsum), rowarg)
    return pr, w, rowarg


def _expand2(x, e2_ref):
    hi, lo = _split_bf16(x, 2)
    return _dot(jnp.concatenate([hi, lo], axis=1), e2_ref[...])


def _ssd_bwd_state_step(sb, xs_c, b_c, wb_c, decb_row):
    out = []
    for g in range(SSD_N_GROUPS):
        gl = slice(g * SSD_GROUP_W, (g + 1) * SSD_GROUP_W)
        xd = xs_c[:, gl] * wb_c[:, gl]
        bg = b_c[:, g * SSD_D_STATE:(g + 1) * SSD_D_STATE]
        out.append(decb_row[:, gl] * sb[g] + _dot_tn(bg, xd))
    return out


def _ssd_main_kernel(xc_ref, z_ref, ga_ref, small_ref, sbin_ref,
                     bias_ref, nega_ref, tri_ref, ones_ref, ef2_ref, eb2_ref,
                     drow_ref, normw_ref, wout_ref, o_ref,
                     rowarg_ref, prf_ref, prb_ref, wf_ref, wb_ref, exf_ref, exb_ref,
                     sf_ref, sbc_ref, y_ref):
    t = pl.program_id(1)
    tile = SSD_TILE
    q = SSD_Q
    nc = tile // q
    b_off = SSD_D_INNER
    c_off = SSD_D_INNER + SSD_BC_W

    @pl.when(t == 0)
    def _():
        sf_ref[...] = jnp.zeros_like(sf_ref)

    pr, w, rowarg = _ssd_small(small_ref, bias_ref, nega_ref, tri_ref, ones_ref)
    rowarg_ref[...] = rowarg
    wbf = w.astype(BF16)
    wf_ref[...] = _dot(wbf, ef2_ref[0:LANES, :]).astype(BF16)
    wb_ref[...] = _dot(wbf, eb2_ref[0:LANES, :]).astype(BF16)
    prf = _expand2(pr, ef2_ref)
    prf_ref[...] = prf
    exf_ref[...] = jnp.exp(prf)
    prb = _expand2(pr, eb2_ref)
    prb_ref[...] = prb
    exb_ref[...] = jnp.exp(prb)

    sbc_ref[nc - 1] = sbin_ref[0, 0]

    for c in range(nc - 1, 0, -1):
        r0 = c * q
        rows = slice(r0, r0 + q)
        sb = [sbc_ref[c, g] for g in range(SSD_N_GROUPS)]
        new = _ssd_bwd_state_step(sb, xc_ref[rows, 0:SSD_D_INNER], xc_ref[rows, b_off:c_off],
                                  wb_ref[rows, :], exb_ref[r0:r0 + 1, :])
        for g in range(SSD_N_GROUPS):
            sbc_ref[c - 1, g] = new[g]

    ii = lax.broadcasted_iota(jnp.int32, (q, LANES), 0)
    ll = lax.broadcasted_iota(jnp.int32, (q, LANES), 1)
    jj = jnp.where(ll < SSD_HEAD_DIM, ll, ll - SSD_HEAD_DIM)
    lower = ii > jj
    upper = ii < jj
    left = ll < SSD_HEAD_DIM
    left_row = lax.broadcasted_iota(jnp.int32, (1, LANES), 1) < SSD_HEAD_DIM
    hpg = SSD_N_HEADS // SSD_N_GROUPS

    for k in range(nc // 2):
        k0 = k * 2 * q
        row_t = rowarg_ref[k0:k0 + 2 * q, :].T
        row_tr = pltpu.roll(row_t, q, 1)
        for s in range(2):
            r0 = k0 + s * q
            rows = slice(r0, r0 + q)
            c = 2 * k + s
            lo_src, hi_src = (row_t, row_tr) if s == 0 else (row_tr, row_t)

            def pair_row(base, h0):
                return jnp.where(left_row, lo_src[base + h0:base + h0 + 1, :],
                                 hi_src[base + h0 + 1:base + h0 + 2, :])

            for g in range(SSD_N_GROUPS):
                gl = slice(g * SSD_GROUP_W, (g + 1) * SSD_GROUP_W)
                b_g = xc_ref[rows, b_off + g * SSD_D_STATE:b_off + (g + 1) * SSD_D_STATE]
                c_g = xc_ref[rows, c_off + g * SSD_D_STATE:c_off + (g + 1) * SSD_D_STATE]
                cb2 = _dot_nt(c_g, jnp.concatenate([b_g, b_g], axis=0))
                sf = sf_ref[g]
                y_off = (_dot(c_g, sf.astype(BF16)) * exf_ref[rows, gl]
                         + _dot(c_g, sbc_ref[c, g].astype(BF16)) * exb_ref[rows, gl])
                xs_g = xc_ref[rows, gl].astype(F32)
                for pm in range(hpg // 2):
                    h0 = g * hpg + 2 * pm
                    cl = slice(g * SSD_GROUP_W + pm * LANES, g * SSD_GROUP_W + (pm + 1) * LANES)
                    arg = jnp.where(
                        lower, prf_ref[rows, cl] + pair_row(0, h0),
                        jnp.where(upper, prb_ref[rows, cl] + pair_row(SSD_N_HEADS, h0),
                                  pair_row(2 * SSD_N_HEADS, h0)))
                    m2 = (cb2 * jnp.exp(arg)).astype(BF16)
                    x_pair = xs_g[:, pm * LANES:(pm + 1) * LANES]
                    x2 = jnp.concatenate([jnp.where(left, x_pair, 0.0), jnp.where(left, 0.0, x_pair)],
                                         axis=0).astype(BF16)
                    y_ref[rows, cl] = (_dot(m2, x2) + y_off[:, pm * LANES:(pm + 1) * LANES]
                                       + drow_ref[:, cl] * x_pair)
                xd = xc_ref[rows, gl] * wf_ref[rows, gl]
                decf = exf_ref[r0 + q - 1:r0 + q, gl]
                sf_ref[g] = decf * sf + _dot_tn(b_g, xd)

    z = z_ref[...].astype(F32)
    y = y_ref[...] * _silu(z)
    outs = []
    for g in range(SSD_N_GROUPS):
        gl = slice(g * SSD_GROUP_W, (g + 1) * SSD_GROUP_W)
        yg = y[:, gl]
        ms = jnp.mean(yg * yg, axis=-1, keepdims=True)
        outs.append((yg * lax.rsqrt(ms + EPS) * normw_ref[:, gl]).astype(BF16))
    yn = jnp.concatenate(outs, axis=1)
    o_ref[...] = (_sigmoid(ga_ref[...].astype(F32)) * _dot(yn, wout_ref[...])).astype(o_ref.dtype)


def _const_spec(arr):
    nd = arr.ndim
    return pl.BlockSpec(arr.shape, lambda b, t: (0,) * nd)


def _ssd(big, small, sb_in, batch, seq_len, wts):
    nt = batch * seq_len
    tile = SSD_TILE
    n_tiles = seq_len // tile
    nc = tile // SSD_Q
    state_shape = (SSD_N_GROUPS, SSD_D_STATE, SSD_GROUP_W)

    consts = [wts["ssd_bias"], wts["ssd_nega"], wts["ssd_tri"], wts["ssd_ones"], wts["ssd_ef2"], wts["ssd_eb2"],
              wts["ssd_drow"], wts["ssd_normw"], wts["w_ssd_out"]]
    z_block = (SSD_XBC + 2 * GLA_DK + 2 * GLA_DV) // SSD_D_INNER
    gate_a_block = (BIG_W - 2 * D_MODEL) // D_MODEL
    wide = pltpu.VMEM((tile, SSD_D_INNER), F32)
    wide16 = pltpu.VMEM((tile, SSD_D_INNER), BF16)
    return pl.pallas_call(
        _ssd_main_kernel,
        grid=(batch, n_tiles),
        in_specs=[pl.BlockSpec((tile, SSD_XBC), lambda b, t: (b * n_tiles + t, 0)),
                  pl.BlockSpec((tile, SSD_D_INNER), lambda b, t: (b * n_tiles + t, z_block)),
                  pl.BlockSpec((tile, D_MODEL), lambda b, t: (b * n_tiles + t, gate_a_block)),
                  pl.BlockSpec((tile, SMALL_W), lambda b, t: (b * n_tiles + t, 0)),
                  pl.BlockSpec((1, 1) + state_shape, lambda b, t: (b, t, 0, 0, 0))]
        + [_const_spec(c) for c in consts],
        out_specs=pl.BlockSpec((tile, D_MODEL), lambda b, t: (b * n_tiles + t, 0)),
        out_shape=jax.ShapeDtypeStruct((nt, D_MODEL), BF16),
        scratch_shapes=[pltpu.VMEM((tile, SMALL_W), F32),
                        wide, wide, wide16, wide16, wide, wide,
                        pltpu.VMEM(state_shape, F32),
                        pltpu.VMEM((nc,) + state_shape, F32),
                        wide],
        compiler_params=_cparams(2),
        name="ssd_main",
    )(big, big, big, small, sb_in, *consts)


def _gla_log_alpha(small_ref, w2, gb):
    pre = _dot(small_ref[...].astype(BF16), w2) + gb
    return -(jnp.maximum(-pre, 0.0) + jnp.log(1.0 + jnp.exp(-jnp.abs(pre)))) * (1.0 / GLA_GATE_NORM)


def _gla_bwd_state_step(sb_h, k_h, v_h, cs_h):
    tot = cs_h[0:1, :]
    k_dec = (k_h * jnp.exp(tot - cs_h)).astype(BF16)
    return jnp.exp(tot) * sb_h + _dot_tn(v_h, k_dec)


def _gla_main_kernel(qkvg_ref, small_ref, sbin_ref, x_ref, a_ref, gateb_ref,
                     w2_ref, gb_ref, tril_ref, triu_ref, normw_ref, wout_ref, wo_ref, o_ref,
                     cs_ref, sf_ref, sbc_ref, oacc_ref):
    t = pl.program_id(1)
    tile = GLA_TILE
    q = GLA_CHUNK
    nc = tile // q
    scale = GLA_HEAD_K ** -0.5

    @pl.when(t == 0)
    def _():
        sf_ref[...] = jnp.zeros_like(sf_ref)

    la = _gla_log_alpha(small_ref, w2_ref[...], gb_ref[...])
    cs_ref[:, 0:GLA_DK] = _dot_exact_lhs(tril_ref[...], la[:, 0:GLA_DK], 2)
    cs_ref[:, GLA_DK:2 * GLA_DK] = _dot_exact_lhs(triu_ref[...], la[:, GLA_DK:2 * GLA_DK], 2)

    sbc_ref[nc - 1] = sbin_ref[0, 0]

    for c in range(nc - 1, 0, -1):
        rows = slice(c * q, (c + 1) * q)
        for h in range(GLA_N_HEADS):
            kl = slice(GLA_DK + h * GLA_HEAD_K, GLA_DK + (h + 1) * GLA_HEAD_K)
            k_h = qkvg_ref[rows, kl].astype(F32)
            v_h = qkvg_ref[rows, 2 * GLA_DK + h * GLA_HEAD_V:2 * GLA_DK + (h + 1) * GLA_HEAD_V]
            sbc_ref[c - 1, h] = _gla_bwd_state_step(sbc_ref[c, h], k_h, v_h, cs_ref[rows, kl])

    ii = lax.broadcasted_iota(jnp.int32, (q, q), 0)
    jj = lax.broadcasted_iota(jnp.int32, (q, q), 1)
    lower_inc = ii >= jj
    upper_inc = ii <= jj

    for c in range(nc):
        rows = slice(c * q, (c + 1) * q)
        for h in range(GLA_N_HEADS):
            fl = slice(h * GLA_HEAD_K, (h + 1) * GLA_HEAD_K)
            bl = slice(GLA_DK + h * GLA_HEAD_K, GLA_DK + (h + 1) * GLA_HEAD_K)
            q_h = qkvg_ref[rows, fl].astype(F32) * scale
            k_h = qkvg_ref[rows, bl].astype(F32)
            v_h = qkvg_ref[rows, 2 * GLA_DK + h * GLA_HEAD_V:2 * GLA_DK + (h + 1) * GLA_HEAD_V]
            csf = cs_ref[rows, fl]
            csb = cs_ref[rows, bl]
            qf = (q_h * jnp.exp(csf)).astype(BF16)
            kf = (k_h * jnp.exp(-csf)).astype(BF16)
            qb = (q_h * jnp.exp(csb)).astype(BF16)
            kb = (k_h * jnp.exp(-csb)).astype(BF16)
            att = (jnp.where(lower_inc, _dot_nt(qf, kf), 0.0)
                   + jnp.where(upper_inc, _dot_nt(qb, kb), 0.0))
            sf = sf_ref[h]
            s_cat = jnp.concatenate([sf, sbc_ref[c, h]], axis=1).astype(BF16)
            o = _dot(att.astype(BF16), v_h) + _dot_nt(jnp.concatenate([qf, qb], axis=1), s_cat)
            oacc_ref[rows, h * GLA_HEAD_V:(h + 1) * GLA_HEAD_V] = o
            tot = csf[q - 1:q, :]
            k_dec = (k_h * jnp.exp(tot - csf)).astype(BF16)
            sf_ref[h] = jnp.exp(tot) * sf + _dot_tn(v_h, k_dec)

    g = qkvg_ref[:, 2 * GLA_DK + GLA_DV:2 * GLA_DK + 2 * GLA_DV].astype(F32)
    o = oacc_ref[...]
    outs = []
    for h in range(GLA_N_HEADS):
        hl = slice(h * GLA_HEAD_V, (h + 1) * GLA_HEAD_V)
        oh = o[:, hl]
        ms = jnp.mean(oh * oh, axis=-1, keepdims=True)
        outs.append((oh * lax.rsqrt(ms + EPS) * normw_ref[:, hl] * _silu(g[:, hl])).astype(BF16))
    branch_b = _dot(jnp.concatenate(outs, axis=1), wout_ref[...])
    mixed = a_ref[...].astype(F32) + _sigmoid(gateb_ref[...].astype(F32)) * branch_b
    o_ref[...] = x_ref[...] + _dot(mixed.astype(BF16), wo_ref[...])


def _gla(big, small, sb_in, x, a_gated, batch, seq_len, wts):
    nt = batch * seq_len
    tile = GLA_TILE
    n_tiles = seq_len // tile
    nc = tile // GLA_CHUNK
    qkvg_w = 2 * GLA_DK + 2 * GLA_DV
    qkvg_block = SSD_XBC // qkvg_w
    gate_b_block = (BIG_W - D_MODEL) // D_MODEL
    state_shape = (GLA_N_HEADS, GLA_HEAD_V, GLA_HEAD_K)

    consts = [wts["gla_w2"], wts["gla_gb"], wts["gla_tril"], wts["gla_triu"], wts["gla_normw"],
              wts["w_gla_out"], wts["w_o"]]
    row = lambda b, t: (b * n_tiles + t, 0)
    return pl.pallas_call(
        _gla_main_kernel,
        grid=(batch, n_tiles),
        in_specs=[pl.BlockSpec((tile, qkvg_w), lambda b, t: (b * n_tiles + t, qkvg_block)),
                  pl.BlockSpec((tile, SMALL_W), row),
                  pl.BlockSpec((1, 1) + state_shape, lambda b, t: (b, t, 0, 0, 0)),
                  pl.BlockSpec((tile, D_MODEL), row),
                  pl.BlockSpec((tile, D_MODEL), row),
                  pl.BlockSpec((tile, D_MODEL), lambda b, t: (b * n_tiles + t, gate_b_block))]
        + [_const_spec(c) for c in consts],
        out_specs=pl.BlockSpec((tile, D_MODEL), row),
        out_shape=jax.ShapeDtypeStruct((nt, D_MODEL), F32),
        scratch_shapes=[pltpu.VMEM((tile, 2 * GLA_DK), F32),
                        pltpu.VMEM(state_shape, F32),
                        pltpu.VMEM((nc,) + state_shape, F32),
                        pltpu.VMEM((tile, GLA_DV), F32)],
        compiler_params=_cparams(2),
        name="gla_main",
    )(big, small, sb_in, x, a_gated, big, *consts)


def _ffn_kernel(prev_ref, cur_ref, next_ref, n2_ref, wg_ref, wv_ref, cwg_ref, cwv_ref, cbg_ref, cbv_ref,
                wd_ref, fn_ref, o_ref, h_ref, ug_ref, uv_ref, acc_ref, *, n_tiles):
    t = pl.program_id(1)
    f = pl.program_id(2)
    nf = pl.num_programs(2)
    tile = FFN_TILE

    def norm_rows(x):
        ms = jnp.mean(x * x, axis=-1, keepdims=True)
        return x * lax.rsqrt(ms + EPS) * n2_ref[...]

    @pl.when(f == 0)
    def _():
        h_ref[0:HALO, :] = norm_rows(prev_ref[...])
        h_ref[HALO:HALO + tile, :] = norm_rows(cur_ref[...])
        h_ref[HALO + tile:HALO + tile + HALO, :] = norm_rows(next_ref[...])
        acc_ref[...] = jnp.zeros_like(acc_ref)

    h = h_ref[...].astype(BF16)
    row = lax.broadcasted_iota(jnp.int32, (tile + 2 * HALO, 1), 0)
    keep = jnp.logical_and(jnp.logical_or(row >= HALO, t > 0),
                           jnp.logical_or(row < HALO + tile, t < n_tiles - 1))
    ug_ref[...] = jnp.where(keep, _dot(h, wg_ref[...]), 0.0)
    uv_ref[...] = jnp.where(keep, _dot(h, wv_ref[...]), 0.0)

    def conv3(u_ref, cw_ref, cb_ref):
        acc = None
        for k in range(3):
            term = u_ref[HALO - 1 + k:HALO - 1 + k + tile, :] * cw_ref[k:k + 1, :]
            acc = term if acc is None else acc + term
        return acc + cb_ref[...]

    act = _silu(conv3(ug_ref, cwg_ref, cbg_ref)) * conv3(uv_ref, cwv_ref, cbv_ref)
    acc_ref[...] += _dot(act.astype(BF16), wd_ref[...])

    @pl.when(f == nf - 1)
    def _():
        x2 = cur_ref[...] + acc_ref[...]
        ms = jnp.mean(x2 * x2, axis=-1, keepdims=True)
        o_ref[...] = x2 * lax.rsqrt(ms + EPS) * fn_ref[...]


def _ffn(x1, batch, seq_len, wts):
    nt = batch * seq_len
    tile = FFN_TILE
    n_tiles = seq_len // tile
    nf = D_FF // FFN_F
    rb = tile // HALO
    seq_rb = seq_len // HALO
    last_rb = nt // HALO - 1
    c2 = lambda b, t, f: (0, 0)
    return pl.pallas_call(
        functools.partial(_ffn_kernel, n_tiles=n_tiles),
        grid=(batch, n_tiles, nf),
        in_specs=[pl.BlockSpec((HALO, D_MODEL), lambda b, t, f: (jnp.maximum(b * seq_rb + t * rb - 1, 0), 0)),
                  pl.BlockSpec((tile, D_MODEL), lambda b, t, f: (b * n_tiles + t, 0)),
                  pl.BlockSpec((HALO, D_MODEL),
                               lambda b, t, f: (jnp.minimum(b * seq_rb + (t + 1) * rb, last_rb), 0)),
                  pl.BlockSpec((1, D_MODEL), c2),
                  pl.BlockSpec((D_MODEL, FFN_F), lambda b, t, f: (0, f)),
                  pl.BlockSpec((D_MODEL, FFN_F), lambda b, t, f: (0, nf + f)),
                  pl.BlockSpec((3, FFN_F), lambda b, t, f: (0, f)),
                  pl.BlockSpec((3, FFN_F), lambda b, t, f: (0, nf + f)),
                  pl.BlockSpec((1, FFN_F), lambda b, t, f: (0, f)),
                  pl.BlockSpec((1, FFN_F), lambda b, t, f: (0, nf + f)),
                  pl.BlockSpec((FFN_F, D_MODEL), lambda b, t, f: (f, 0)),
                  pl.BlockSpec((1, D_MODEL), c2)],
        out_specs=pl.BlockSpec((tile, D_MODEL), lambda b, t, f: (b * n_tiles + t, 0)),
        out_shape=jax.ShapeDtypeStruct((nt, D_MODEL), F32),
        scratch_shapes=[pltpu.VMEM((tile + 2 * HALO, D_MODEL), F32),
                        pltpu.VMEM((tile + 2 * HALO, FFN_F), F32),
                        pltpu.VMEM((tile + 2 * HALO, FFN_F), F32),
                        pltpu.VMEM((tile, D_MODEL), F32)],
        compiler_params=_cparams(3),
        name="ffn",
    )(x1, x1, x1, wts["norm2_w"], wts["w_ffn_up"], wts["w_ffn_up"], wts["ffn_conv_w"], wts["ffn_conv_w"],
      wts["ffn_conv_b"], wts["ffn_conv_b"], wts["w_ffn_down"], wts["final_norm_w"])


def _block_tri(n, blk, kind):
    i = np.arange(n)[:, None]
    j = np.arange(n)[None, :]
    m = (i // blk) == (j // blk)
    if kind == "lower":
        m = m & (i >= j)
    elif kind == "upper":
        m = m & (i <= j)
    return jnp.asarray(m, dtype=BF16)


def _prep_weights(norm1_w, w_in, ssd_conv_w, ssd_conv_b, ssd_a_log, ssd_dt_bias, ssd_d, ssd_norm_w, w_ssd_out,
                  gla_gate_w2, gla_gate_b, gla_norm_w, w_gla_out, w_o, norm2_w, w_ffn_up, ffn_conv_w, ffn_conv_b,
                  w_ffn_down, final_norm_w):
    o_z = 0
    o_xbc = o_z + SSD_D_INNER
    o_dt = o_xbc + SSD_XBC
    o_q = o_dt + 2 * SSD_N_HEADS
    o_k = o_q + GLA_DK
    o_v = o_k + GLA_DK
    o_g = o_v + GLA_DV
    o_gk = o_g + GLA_DV
    o_gate = o_gk + 2 * GLA_GATE_RANK
    w_big = jnp.concatenate([w_in[:, o_xbc:o_dt], w_in[:, o_q:o_gk], w_in[:, o_z:o_xbc], w_in[:, o_gate:]],
                            axis=1).astype(BF16)
    pad = SMALL_W - 2 * SSD_N_HEADS - 2 * GLA_GATE_RANK
    w_small = jnp.concatenate([w_in[:, o_dt:o_q], w_in[:, o_gk:o_gate], jnp.zeros((D_MODEL, pad), F32)],
                              axis=1).astype(BF16)

    def row128(v):
        return jnp.concatenate([v.reshape(-1), jnp.zeros((SMALL_W - v.size,), F32)]).reshape(1, SMALL_W)

    head_of_lane = np.arange(SSD_D_INNER) // SSD_HEAD_DIM
    ef = (np.arange(SMALL_W)[:, None] == head_of_lane[None, :])
    eb = (np.arange(SMALL_W)[:, None] == (head_of_lane[None, :] + SSD_N_HEADS))
    w2 = jnp.zeros((SMALL_W, 2 * GLA_DK), F32)
    w2 = w2.at[GK_OFF:GK_OFF + GLA_GATE_RANK, :GLA_DK].set(gla_gate_w2[0])
    w2 = w2.at[GK_OFF + GLA_GATE_RANK:GK_OFF + 2 * GLA_GATE_RANK, GLA_DK:].set(gla_gate_w2[1])
    return {
        "norm1_w": norm1_w.reshape(1, D_MODEL),
        "w_big": w_big,
        "w_small": w_small,
        "ssd_conv_w": ssd_conv_w,
        "ssd_conv_b": ssd_conv_b.reshape(1, SSD_XBC),
        "ssd_bias": row128(ssd_dt_bias),
        "ssd_nega": row128(-jnp.exp(ssd_a_log)),
        "ssd_tri": _block_tri(SSD_TILE, SSD_Q, "lower"),
        "ssd_ones": _block_tri(SSD_TILE, SSD_Q, "ones"),
        "ssd_tri_full": _block_tri(PRE_TILE, PRE_TILE, "lower"),
        "ssd_ones_full": _block_tri(PRE_TILE, PRE_TILE, "ones"),
        "ssd_ef2": jnp.asarray(np.concatenate([ef, ef], axis=0), dtype=BF16),
        "ssd_eb2": jnp.asarray(np.concatenate([eb, eb], axis=0), dtype=BF16),
        "ssd_drow": jnp.repeat(ssd_d, SSD_HEAD_DIM).reshape(1, SSD_D_INNER),
        "ssd_normw": ssd_norm_w.reshape(1, SSD_D_INNER),
        "w_ssd_out": w_ssd_out.astype(BF16),
        "gla_w2": w2.astype(BF16),
        "gla_gb": gla_gate_b.reshape(1, 2 * GLA_DK),
        "gla_tril": _block_tri(GLA_TILE, GLA_CHUNK, "lower"),
        "gla_triu": _block_tri(GLA_TILE, GLA_CHUNK, "upper"),
        "gla_triu_full": _block_tri(PRE_TILE, PRE_TILE, "upper"),
        "gla_normw": gla_norm_w.reshape(1, GLA_DV),
        "w_gla_out": w_gla_out.astype(BF16),
        "w_o": w_o.astype(BF16),
        "norm2_w": norm2_w.reshape(1, D_MODEL),
        "w_ffn_up": w_ffn_up.astype(BF16),
        "ffn_conv_w": ffn_conv_w,
        "ffn_conv_b": ffn_conv_b.reshape(1, 2 * D_FF),
        "w_ffn_down": w_ffn_down.astype(BF16),
        "final_norm_w": final_norm_w.reshape(1, D_MODEL),
    }


def _trunk(x3, wts):
    batch, seq_len, _ = x3.shape
    x = x3.reshape(batch * seq_len, D_MODEL)
    big, small, ssd_sb, gla_sb = _proj_pre(x, batch, seq_len, wts)
    a_gated = _ssd(big, small, ssd_sb, batch, seq_len, wts)
    x1 = _gla(big, small, gla_sb, x, a_gated, batch, seq_len, wts)
    out = _ffn(x1, batch, seq_len, wts)
    return out.reshape(batch, seq_len, D_MODEL)


def kernel(x_prompt, x_sample, norm1_w, w_in, ssd_conv_w, ssd_conv_b, ssd_a_log, ssd_dt_bias, ssd_d, ssd_norm_w,
           w_ssd_out, gla_gate_w2, gla_gate_b, gla_norm_w, w_gla_out, w_o, norm2_w, w_ffn_up, ffn_conv_w,
           ffn_conv_b, w_ffn_down, final_norm_w):
    wts = _prep_weights(norm1_w[0], w_in[0], ssd_conv_w[0], ssd_conv_b[0], ssd_a_log[0], ssd_dt_bias[0], ssd_d[0],
                        ssd_norm_w[0], w_ssd_out[0], gla_gate_w2[0], gla_gate_b[0], gla_norm_w[0], w_gla_out[0],
                        w_o[0], norm2_w[0], w_ffn_up[0], ffn_conv_w[0], ffn_conv_b[0], w_ffn_down[0], final_norm_w)
    return (_trunk(x_prompt, wts), _trunk(x_sample, wts))
```

```python
import functools

import numpy as np
import jax
import jax.numpy as jnp
from jax import lax
from jax.experimental import pallas as pl
from jax.experimental.pallas import tpu as pltpu

F32 = jnp.float32
BF16 = jnp.bfloat16

D_MODEL = 1024
SSD_D_INNER = 2048
SSD_HEAD_DIM = 64
SSD_N_HEADS = 32
SSD_N_GROUPS = 4
SSD_D_STATE = 128
SSD_GROUP_W = SSD_D_INNER // SSD_N_GROUPS
SSD_BC_W = SSD_N_GROUPS * SSD_D_STATE
SSD_XBC = SSD_D_INNER + 2 * SSD_BC_W
SSD_CONV = 5
GLA_N_HEADS = 4
GLA_DK = 512
GLA_DV = 1024
GLA_HEAD_K = 128
GLA_HEAD_V = 256
GLA_GATE_RANK = 16
GLA_GATE_NORM = 16.0
GLA_CHUNK = 64
D_FF = 2816
EPS = 1e-6

LANES = 128
SUBLANES = 8
VMEM_LIMIT = 56 * 1024 * 1024

BIG_W = SSD_XBC + 2 * GLA_DK + 2 * GLA_DV + SSD_D_INNER + 2 * D_MODEL
SMALL_W = LANES
GK_OFF = 2 * SSD_N_HEADS

PRE_TILE = 256
SSD_TILE = 256
SSD_Q = 64
GLA_TILE = 256
HALO = SUBLANES
SSD_HALO = 2 * SUBLANES
CONV_SLAB = 256
FFN_TILE = 512
FFN_CHUNKS = ((0, 1024), (1024, 2048), (2048, D_FF))
REST_TN = 512


def _dot(a, b):
    return jnp.dot(a, b, preferred_element_type=F32)


def _dot_tn(a, b):
    return lax.dot_general(a, b, (((0,), (0,)), ((), ())), preferred_element_type=F32)


def _dot_nt(a, b):
    return lax.dot_general(a, b, (((1,), (1,)), ((), ())), preferred_element_type=F32)


def _split_bf16(x, n):
    parts = []
    r = x
    for i in range(n):
        p = r.astype(BF16)
        parts.append(p)
        if i + 1 < n:
            r = r - p.astype(F32)
    return parts


def _dot_exact_lhs(m_bf16, x, n=3):
    acc = None
    for p in _split_bf16(x, n):
        t = _dot(m_bf16, p)
        acc = t if acc is None else acc + t
    return acc


def _sigmoid(x):
    return 0.5 * jnp.tanh(0.5 * x) + 0.5


def _silu(x):
    return x * _sigmoid(x)


def _softplus(x):
    return jnp.maximum(x, 0.0) + jnp.log1p(jnp.exp(-jnp.abs(x)))


def _cparams(n_axes):
    return pltpu.CompilerParams(dimension_semantics=("arbitrary",) * n_axes,
                                vmem_limit_bytes=VMEM_LIMIT)


def _proj_pre_kernel(xprev_ref, xcur_ref, xnext_ref, nw_ref, wbig_ref, wsmall_ref,
                     convw_ref, convb_ref, bias_ref, nega_ref, tri_ref, ones_ref, eb2_ref,
                     w2_ref, gb_ref, triu_ref,
                     big_ref, small_ref, ssd_sbin_ref, gla_sbin_ref,
                     h_ref, ext_ref, ssd_sb_ref, gla_sb_ref, *, n_tiles):
    t = pl.program_id(1)
    tile = PRE_TILE
    halo = SSD_HALO
    first = t == n_tiles - 1
    last = t == 0

    @pl.when(t == 0)
    def _():
        ssd_sb_ref[...] = jnp.zeros_like(ssd_sb_ref)
        gla_sb_ref[...] = jnp.zeros_like(gla_sb_ref)

    ssd_sbin_ref[0, 0] = ssd_sb_ref[...]
    gla_sbin_ref[0, 0] = gla_sb_ref[...]

    def norm_rows(x):
        ms = jnp.mean(x * x, axis=-1, keepdims=True)
        return (x * lax.rsqrt(ms + EPS) * nw_ref[...]).astype(BF16)

    h_ref[0:halo, :] = norm_rows(xprev_ref[...])
    h_ref[halo:halo + tile, :] = norm_rows(xcur_ref[...])
    h_ref[halo + tile:halo + tile + halo, :] = norm_rows(xnext_ref[...])

    row = lax.broadcasted_iota(jnp.int32, (tile + 2 * halo, 1), 0)
    keep = jnp.logical_and(jnp.logical_or(row >= halo, jnp.logical_not(first)),
                           jnp.logical_or(row < halo + tile, jnp.logical_not(last)))
    h = h_ref[halo:halo + tile, :]

    def xbc_slab(lo):
        ext_ref[(lo // CONV_SLAB) % 2] = jnp.where(keep, _dot(h_ref[...], wbig_ref[:, lo:lo + CONV_SLAB]), 0.0)

    def conv_slab(lo):
        ext = ext_ref.at[(lo // CONV_SLAB) % 2]
        y = _conv_taps(ext, convw_ref, convb_ref, halo, tile, lo, lo + CONV_SLAB)
        big_ref[:, lo:lo + CONV_SLAB] = _silu(y).astype(BF16)

    def rest_chunk(lo):
        big_ref[:, lo:lo + REST_TN] = _dot(h, wbig_ref[:, lo:lo + REST_TN]).astype(BF16)

    slabs = list(range(0, SSD_XBC, CONV_SLAB))
    rest = list(range(SSD_XBC, BIG_W, REST_TN))
    xbc_slab(slabs[0])
    for i, lo in enumerate(slabs):
        if i + 1 < len(slabs):
            xbc_slab(slabs[i + 1])
        if rest:
            rest_chunk(rest.pop(0))
        conv_slab(lo)
    for lo in rest:
        rest_chunk(lo)
    small_ref[...] = _dot(h, wsmall_ref[...])

    pr, w, _ = _ssd_small(small_ref, bias_ref, nega_ref, tri_ref, ones_ref)
    wb = _dot(w.astype(BF16), eb2_ref[0:LANES, :]).astype(BF16)
    decb = _expand2(jnp.exp(pr[0:SUBLANES, :]), eb2_ref)[0:1, :]
    sb = [ssd_sb_ref[g] for g in range(SSD_N_GROUPS)]
    new = _ssd_bwd_state_step(sb, big_ref[:, 0:SSD_D_INNER],
                              big_ref[:, SSD_D_INNER:SSD_D_INNER + SSD_BC_W], wb, decb)
    for g in range(SSD_N_GROUPS):
        ssd_sb_ref[g] = new[g]

    la = _gla_log_alpha(small_ref, w2_ref[:, GLA_DK:], gb_ref[:, GLA_DK:])
    cs = _dot_exact_lhs(triu_ref[...], la, 2)
    k_off = SSD_XBC + GLA_DK
    v_off = SSD_XBC + 2 * GLA_DK
    for hd in range(GLA_N_HEADS):
        k_h = big_ref[:, k_off + hd * GLA_HEAD_K:k_off + (hd + 1) * GLA_HEAD_K].astype(F32)
        v_h = big_ref[:, v_off + hd * GLA_HEAD_V:v_off + (hd + 1) * GLA_HEAD_V]
        gla_sb_ref[hd] = _gla_bwd_state_step(gla_sb_ref[hd], k_h, v_h,
                                             cs[:, hd * GLA_HEAD_K:(hd + 1) * GLA_HEAD_K])


def _proj_pre(x, batch, seq_len, wts):
    nt = batch * seq_len
    tile = PRE_TILE
    n_tiles = seq_len // tile
    rb = tile // SSD_HALO
    seq_rb = seq_len // SSD_HALO
    last_rb = nt // SSD_HALO - 1
    rev = lambda t: n_tiles - 1 - t
    ssd_state = (SSD_N_GROUPS, SSD_D_STATE, SSD_GROUP_W)
    gla_state = (GLA_N_HEADS, GLA_HEAD_V, GLA_HEAD_K)
    consts = [wts["ssd_conv_w"], wts["ssd_conv_b"], wts["ssd_bias"], wts["ssd_nega"],
              wts["ssd_tri_full"], wts["ssd_ones_full"], wts["ssd_eb2"],
              wts["gla_w2"], wts["gla_gb"], wts["gla_triu_full"]]
    row = lambda b, t: (b * n_tiles + rev(t), 0)
    return pl.pallas_call(
        functools.partial(_proj_pre_kernel, n_tiles=n_tiles),
        grid=(batch, n_tiles),
        in_specs=[pl.BlockSpec((SSD_HALO, D_MODEL),
                               lambda b, t: (jnp.maximum(b * seq_rb + rev(t) * rb - 1, 0), 0)),
                  pl.BlockSpec((tile, D_MODEL), row),
                  pl.BlockSpec((SSD_HALO, D_MODEL),
                               lambda b, t: (jnp.minimum(b * seq_rb + (rev(t) + 1) * rb, last_rb), 0)),
                  pl.BlockSpec((1, D_MODEL), lambda b, t: (0, 0)),
                  pl.BlockSpec((D_MODEL, BIG_W), lambda b, t: (0, 0), pipeline_mode=pl.Buffered(1)),
                  pl.BlockSpec((D_MODEL, SMALL_W), lambda b, t: (0, 0))]
        + [_const_spec(c) for c in consts],
        out_specs=[pl.BlockSpec((tile, BIG_W), row),
                   pl.BlockSpec((tile, SMALL_W), row),
                   pl.BlockSpec((1, 1) + ssd_state, lambda b, t: (b, rev(t), 0, 0, 0)),
                   pl.BlockSpec((1, 1) + gla_state, lambda b, t: (b, rev(t), 0, 0, 0))],
        out_shape=[jax.ShapeDtypeStruct((nt, BIG_W), BF16),
                   jax.ShapeDtypeStruct((nt, SMALL_W), F32),
                   jax.ShapeDtypeStruct((batch, n_tiles) + ssd_state, F32),
                   jax.ShapeDtypeStruct((batch, n_tiles) + gla_state, F32)],
        scratch_shapes=[pltpu.VMEM((tile + 2 * SSD_HALO, D_MODEL), BF16),
                        pltpu.VMEM((2, tile + 2 * SSD_HALO, CONV_SLAB), F32),
                        pltpu.VMEM(ssd_state, F32),
                        pltpu.VMEM(gla_state, F32)],
        compiler_params=_cparams(2),
        name="proj_pre",
    )(x, x, x, wts["norm1_w"], wts["w_big"], wts["w_small"], *consts)


def _conv_taps(ext_ref, w_ref, b_ref, row0, n_rows, lo, hi):
    pad = (SSD_CONV - 1) // 2
    acc = b_ref[:, lo:hi]
    for k in range(SSD_CONV):
        acc = acc + ext_ref[row0 - pad + k:row0 - pad + k + n_rows, :] * w_ref[k:k + 1, lo:hi]
    return acc


def _ssd_small(small_ref, bias_ref, nega_ref, tri_ref, ones_ref):
    s = small_ref[...]
    dt = _softplus(s + bias_ref[...])
    a = dt * nega_ref[...]
    p = _dot_exact_lhs(tri_ref[...], a)
    tot = _dot_exact_lhs(ones_ref[...], a)
    col = lax.broadcasted_iota(jnp.int32, s.shape, 1)
    fwd = col < SSD_N_HEADS
    r = tot - p + a
    pr = jnp.where(fwd, p, r)
    w = dt * jnp.exp(jnp.where(fwd, tot - p, p - a))
    rowarg = jnp.log(dt) - pr
    dsum = pltpu.roll(dt, 2 * SSD_N_HEADS, 1) + pltpu.roll(dt, SSD_N_HEADS, 1)
    diag_cols = (col >= 2 * SSD_N_HEADS) & (col < 3 * SSD_N_HEADS)
    rowarg = jnp.where(diag_cols, jnp.log(dsum), rowarg)
    return pr, w, rowarg


def _expand2(x, e2_ref):
    hi, lo = _split_bf16(x, 2)
    return _dot(jnp.concatenate([hi, lo], axis=1), e2_ref[...])


def _ssd_bwd_state_step(sb, xs_c, b_c, wb_c, decb_row):
    out = []
    for g in range(SSD_N_GROUPS):
        gl = slice(g * SSD_GROUP_W, (g + 1) * SSD_GROUP_W)
        xd = xs_c[:, gl] * wb_c[:, gl]
        bg = b_c[:, g * SSD_D_STATE:(g + 1) * SSD_D_STATE]
        out.append(decb_row[:, gl] * sb[g] + _dot_tn(bg, xd))
    return out


def _ssd_main_kernel(xc_ref, z_ref, ga_ref, small_ref, sbin_ref,
                     bias_ref, nega_ref, tri_ref, ones_ref, ef2_ref, eb2_ref,
                     drow_ref, normw_ref, wout_ref, o_ref,
                     rowarg_ref, prf_ref, prb_ref, wf_ref, wb_ref, exf_ref, exb_ref,
                     sf_ref, sbc_ref, y_ref):
    t = pl.program_id(1)
    tile = SSD_TILE
    q = SSD_Q
    nc = tile // q
    b_off = SSD_D_INNER
    c_off = SSD_D_INNER + SSD_BC_W

    @pl.when(t == 0)
    def _():
        sf_ref[...] = jnp.zeros_like(sf_ref)

    pr, w, rowarg = _ssd_small(small_ref, bias_ref, nega_ref, tri_ref, ones_ref)
    rowarg_ref[...] = rowarg
    wbf = w.astype(BF16)
    wf_ref[...] = _dot(wbf, ef2_ref[0:LANES, :]).astype(BF16)
    wb_ref[...] = _dot(wbf, eb2_ref[0:LANES, :]).astype(BF16)
    prf = _expand2(pr, ef2_ref)
    prf_ref[...] = prf
    exf_ref[...] = jnp.exp(prf)
    prb = _expand2(pr, eb2_ref)
    prb_ref[...] = prb
    exb_ref[...] = jnp.exp(prb)

    sbc_ref[nc - 1] = sbin_ref[0, 0]

    for c in range(nc - 1, 0, -1):
        r0 = c * q
        rows = slice(r0, r0 + q)
        sb = [sbc_ref[c, g] for g in range(SSD_N_GROUPS)]
        new = _ssd_bwd_state_step(sb, xc_ref[rows, 0:SSD_D_INNER], xc_ref[rows, b_off:c_off],
                                  wb_ref[rows, :], exb_ref[r0:r0 + 1, :])
        for g in range(SSD_N_GROUPS):
            sbc_ref[c - 1, g] = new[g]

    ii = lax.broadcasted_iota(jnp.int32, (q, LANES), 0)
    ll = lax.broadcasted_iota(jnp.int32, (q, LANES), 1)
    jj = jnp.where(ll < SSD_HEAD_DIM, ll, ll - SSD_HEAD_DIM)
    lower = ii > jj
    upper = ii < jj
    left = ll < SSD_HEAD_DIM
    left_row = lax.broadcasted_iota(jnp.int32, (1, LANES), 1) < SSD_HEAD_DIM
    hpg = SSD_N_HEADS // SSD_N_GROUPS

    for k in range(nc // 2):
        k0 = k * 2 * q
        row_t = rowarg_ref[k0:k0 + 2 * q, :].T
        row_tr = pltpu.roll(row_t, q, 1)
        for s in range(2):
            r0 = k0 + s * q
            rows = slice(r0, r0 + q)
            c = 2 * k + s
            lo_src, hi_src = (row_t, row_tr) if s == 0 else (row_tr, row_t)

            def pair_row(base, h0):
                return jnp.where(left_row, lo_src[base + h0:base + h0 + 1, :],
                                 hi_src[base + h0 + 1:base + h0 + 2, :])

            for g in range(SSD_N_GROUPS):
                gl = slice(g * SSD_GROUP_W, (g + 1) * SSD_GROUP_W)
                b_g = xc_ref[rows, b_off + g * SSD_D_STATE:b_off + (g + 1) * SSD_D_STATE]
                c_g = xc_ref[rows, c_off + g * SSD_D_STATE:c_off + (g + 1) * SSD_D_STATE]
                cb2 = _dot_nt(c_g, jnp.concatenate([b_g, b_g], axis=0))
                sf = sf_ref[g]
                y_off = (_dot(c_g, sf.astype(BF16)) * exf_ref[rows, gl]
                         + _dot(c_g, sbc_ref[c, g].astype(BF16)) * exb_ref[rows, gl])
                xs_g = xc_ref[rows, gl].astype(F32)
                for pm in range(hpg // 2):
                    h0 = g * hpg + 2 * pm
                    cl = slice(g * SSD_GROUP_W + pm * LANES, g * SSD_GROUP_W + (pm + 1) * LANES)
                    arg = jnp.where(
                        lower, prf_ref[rows, cl] + pair_row(0, h0),
                        jnp.where(upper, prb_ref[rows, cl] + pair_row(SSD_N_HEADS, h0),
                                  pair_row(2 * SSD_N_HEADS, h0)))
                    m2 = (cb2 * jnp.exp(arg)).astype(BF16)
                    x_pair = xs_g[:, pm * LANES:(pm + 1) * LANES]
                    x2 = jnp.concatenate([jnp.where(left, x_pair, 0.0), jnp.where(left, 0.0, x_pair)],
                                         axis=0).astype(BF16)
                    y_ref[rows, cl] = (_dot(m2, x2) + y_off[:, pm * LANES:(pm + 1) * LANES]
                                       + drow_ref[:, cl] * x_pair)
                xd = xc_ref[rows, gl] * wf_ref[rows, gl]
                decf = exf_ref[r0 + q - 1:r0 + q, gl]
                sf_ref[g] = decf * sf + _dot_tn(b_g, xd)

    z = z_ref[...].astype(F32)
    y = y_ref[...] * _silu(z)
    outs = []
    for g in range(SSD_N_GROUPS):
        gl = slice(g * SSD_GROUP_W, (g + 1) * SSD_GROUP_W)
        yg = y[:, gl]
        ms = jnp.mean(yg * yg, axis=-1, keepdims=True)
        outs.append((yg * lax.rsqrt(ms + EPS) * normw_ref[:, gl]).astype(BF16))
    yn = jnp.concatenate(outs, axis=1)
    o_ref[...] = (_sigmoid(ga_ref[...].astype(F32)) * _dot(yn, wout_ref[...])).astype(o_ref.dtype)


def _const_spec(arr):
    nd = arr.ndim
    return pl.BlockSpec(arr.shape, lambda b, t: (0,) * nd)


def _ssd(big, small, sb_in, batch, seq_len, wts):
    nt = batch * seq_len
    tile = SSD_TILE
    n_tiles = seq_len // tile
    nc = tile // SSD_Q
    state_shape = (SSD_N_GROUPS, SSD_D_STATE, SSD_GROUP_W)

    consts = [wts["ssd_bias"], wts["ssd_nega"], wts["ssd_tri"], wts["ssd_ones"], wts["ssd_ef2"], wts["ssd_eb2"],
              wts["ssd_drow"], wts["ssd_normw"], wts["w_ssd_out"]]
    z_block = (SSD_XBC + 2 * GLA_DK + 2 * GLA_DV) // SSD_D_INNER
    gate_a_block = (BIG_W - 2 * D_MODEL) // D_MODEL
    wide = pltpu.VMEM((tile, SSD_D_INNER), F32)
    wide16 = pltpu.VMEM((tile, SSD_D_INNER), BF16)
    return pl.pallas_call(
        _ssd_main_kernel,
        grid=(batch, n_tiles),
        in_specs=[pl.BlockSpec((tile, SSD_XBC), lambda b, t: (b * n_tiles + t, 0)),
                  pl.BlockSpec((tile, SSD_D_INNER), lambda b, t: (b * n_tiles + t, z_block)),
                  pl.BlockSpec((tile, D_MODEL), lambda b, t: (b * n_tiles + t, gate_a_block)),
                  pl.BlockSpec((tile, SMALL_W), lambda b, t: (b * n_tiles + t, 0)),
                  pl.BlockSpec((1, 1) + state_shape, lambda b, t: (b, t, 0, 0, 0))]
        + [_const_spec(c) for c in consts],
        out_specs=pl.BlockSpec((tile, D_MODEL), lambda b, t: (b * n_tiles + t, 0)),
        out_shape=jax.ShapeDtypeStruct((nt, D_MODEL), BF16),
        scratch_shapes=[pltpu.VMEM((tile, SMALL_W), F32),
                        wide, wide, wide16, wide16, wide, wide,
                        pltpu.VMEM(state_shape, F32),
                        pltpu.VMEM((nc,) + state_shape, F32),
                        wide],
        compiler_params=_cparams(2),
        name="ssd_main",
    )(big, big, big, small, sb_in, *consts)


def _gla_log_alpha(small_ref, w2, gb):
    pre = _dot(small_ref[...].astype(BF16), w2) + gb
    return -(jnp.maximum(-pre, 0.0) + jnp.log(1.0 + jnp.exp(-jnp.abs(pre)))) * (1.0 / GLA_GATE_NORM)


def _gla_bwd_state_step(sb_h, k_h, v_h, cs_h):
    tot = cs_h[0:1, :]
    k_dec = (k_h * jnp.exp(tot - cs_h)).astype(BF16)
    return jnp.exp(tot) * sb_h + _dot_tn(v_h, k_dec)


def _gla_main_kernel(qkvg_ref, small_ref, sbin_ref, x_ref, a_ref, gateb_ref,
                     w2_ref, gb_ref, tril_ref, triu_ref, normw_ref, wout_ref, wo_ref, o_ref,
                     cs_ref, sf_ref, sbc_ref, oacc_ref):
    t = pl.program_id(1)
    tile = GLA_TILE
    q = GLA_CHUNK
    nc = tile // q
    scale = GLA_HEAD_K ** -0.5

    @pl.when(t == 0)
    def _():
        sf_ref[...] = jnp.zeros_like(sf_ref)

    la = _gla_log_alpha(small_ref, w2_ref[...], gb_ref[...])
    cs_ref[:, 0:GLA_DK] = _dot_exact_lhs(tril_ref[...], la[:, 0:GLA_DK], 2)
    cs_ref[:, GLA_DK:2 * GLA_DK] = _dot_exact_lhs(triu_ref[...], la[:, GLA_DK:2 * GLA_DK], 2)

    sbc_ref[nc - 1] = sbin_ref[0, 0]

    for c in range(nc - 1, 0, -1):
        rows = slice(c * q, (c + 1) * q)
        for h in range(GLA_N_HEADS):
            kl = slice(GLA_DK + h * GLA_HEAD_K, GLA_DK + (h + 1) * GLA_HEAD_K)
            k_h = qkvg_ref[rows, kl].astype(F32)
            v_h = qkvg_ref[rows, 2 * GLA_DK + h * GLA_HEAD_V:2 * GLA_DK + (h + 1) * GLA_HEAD_V]
            sbc_ref[c - 1, h] = _gla_bwd_state_step(sbc_ref[c, h], k_h, v_h, cs_ref[rows, kl])

    ii = lax.broadcasted_iota(jnp.int32, (q, q), 0)
    jj = lax.broadcasted_iota(jnp.int32, (q, q), 1)
    lower_inc = ii >= jj
    upper_inc = ii <= jj

    for c in range(nc):
        rows = slice(c * q, (c + 1) * q)
        for h in range(GLA_N_HEADS):
            fl = slice(h * GLA_HEAD_K, (h + 1) * GLA_HEAD_K)
            bl = slice(GLA_DK + h * GLA_HEAD_K, GLA_DK + (h + 1) * GLA_HEAD_K)
            q_h = qkvg_ref[rows, fl].astype(F32) * scale
            k_h = qkvg_ref[rows, bl].astype(F32)
            v_h = qkvg_ref[rows, 2 * GLA_DK + h * GLA_HEAD_V:2 * GLA_DK + (h + 1) * GLA_HEAD_V]
            csf = cs_ref[rows, fl]
            csb = cs_ref[rows, bl]
            qf = (q_h * jnp.exp(csf)).astype(BF16)
            kf = (k_h * jnp.exp(-csf)).astype(BF16)
            qb = (q_h * jnp.exp(csb)).astype(BF16)
            kb = (k_h * jnp.exp(-csb)).astype(BF16)
            att = (jnp.where(lower_inc, _dot_nt(qf, kf), 0.0)
                   + jnp.where(upper_inc, _dot_nt(qb, kb), 0.0))
            sf = sf_ref[h]
            s_cat = jnp.concatenate([sf, sbc_ref[c, h]], axis=1).astype(BF16)
            o = _dot(att.astype(BF16), v_h) + _dot_nt(jnp.concatenate([qf, qb], axis=1), s_cat)
            oacc_ref[rows, h * GLA_HEAD_V:(h + 1) * GLA_HEAD_V] = o
            tot = csf[q - 1:q, :]
            k_dec = (k_h * jnp.exp(tot - csf)).astype(BF16)
            sf_ref[h] = jnp.exp(tot) * sf + _dot_tn(v_h, k_dec)

    g = qkvg_ref[:, 2 * GLA_DK + GLA_DV:2 * GLA_DK + 2 * GLA_DV].astype(F32)
    o = oacc_ref[...]
    outs = []
    for h in range(GLA_N_HEADS):
        hl = slice(h * GLA_HEAD_V, (h + 1) * GLA_HEAD_V)
        oh = o[:, hl]
        ms = jnp.mean(oh * oh, axis=-1, keepdims=True)
        outs.append((oh * lax.rsqrt(ms + EPS) * normw_ref[:, hl] * _silu(g[:, hl])).astype(BF16))
    branch_b = _dot(jnp.concatenate(outs, axis=1), wout_ref[...])
    mixed = a_ref[...].astype(F32) + _sigmoid(gateb_ref[...].astype(F32)) * branch_b
    o_ref[...] = x_ref[...] + _dot(mixed.astype(BF16), wo_ref[...])


def _gla(big, small, sb_in, x, a_gated, batch, seq_len, wts):
    nt = batch * seq_len
    tile = GLA_TILE
    n_tiles = seq_len // tile
    nc = tile // GLA_CHUNK
    qkvg_w = 2 * GLA_DK + 2 * GLA_DV
    qkvg_block = SSD_XBC // qkvg_w
    gate_b_block = (BIG_W - D_MODEL) // D_MODEL
    state_shape = (GLA_N_HEADS, GLA_HEAD_V, GLA_HEAD_K)

    consts = [wts["gla_w2"], wts["gla_gb"], wts["gla_tril"], wts["gla_triu"], wts["gla_normw"],
              wts["w_gla_out"], wts["w_o"]]
    row = lambda b, t: (b * n_tiles + t, 0)
    return pl.pallas_call(
        _gla_main_kernel,
        grid=(batch, n_tiles),
        in_specs=[pl.BlockSpec((tile, qkvg_w), lambda b, t: (b * n_tiles + t, qkvg_block)),
                  pl.BlockSpec((tile, SMALL_W), row),
                  pl.BlockSpec((1, 1) + state_shape, lambda b, t: (b, t, 0, 0, 0)),
                  pl.BlockSpec((tile, D_MODEL), row),
                  pl.BlockSpec((tile, D_MODEL), row),
                  pl.BlockSpec((tile, D_MODEL), lambda b, t: (b * n_tiles + t, gate_b_block))]
        + [_const_spec(c) for c in consts],
        out_specs=pl.BlockSpec((tile, D_MODEL), row),
        out_shape=jax.ShapeDtypeStruct((nt, D_MODEL), F32),
        scratch_shapes=[pltpu.VMEM((tile, 2 * GLA_DK), F32),
                        pltpu.VMEM(state_shape, F32),
                        pltpu.VMEM((nc,) + state_shape, F32),
                        pltpu.VMEM((tile, GLA_DV), F32)],
        compiler_params=_cparams(2),
        name="gla_main",
    )(big, small, sb_in, x, a_gated, big, *consts)


def _mix_kernel(xc_ref, qkvg_ref, z_ref, gates_ref, small_ref, ssd_sbin_ref, gla_sbin_ref, x_ref,
                bias_ref, nega_ref, tri_ref, ones_ref, ef2_ref, eb2_ref, drow_ref, snormw_ref, swout_ref,
                w2_ref, gb_ref, tril_ref, triu_ref, gnormw_ref, gwout_ref, wo_ref,
                o_ref,
                rowarg_ref, prf_ref, prb_ref, wf_ref, wb_ref, exf_ref, exb_ref, sf_ref, sbc_ref, y_ref,
                cs_ref, gsf_ref, gsbc_ref, oacc_ref):
    t = pl.program_id(1)
    tile = SSD_TILE
    q = SSD_Q
    nc = tile // q
    b_off = SSD_D_INNER
    c_off = SSD_D_INNER + SSD_BC_W
    scale = GLA_HEAD_K ** -0.5
    hpg = SSD_N_HEADS // SSD_N_GROUPS

    @pl.when(t == 0)
    def _():
        sf_ref[...] = jnp.zeros_like(sf_ref)
        gsf_ref[...] = jnp.zeros_like(gsf_ref)

    def ssd_prep():
        pr, w, rowarg = _ssd_small(small_ref, bias_ref, nega_ref, tri_ref, ones_ref)
        rowarg_ref[...] = rowarg
        wbf = w.astype(BF16)
        wf_ref[...] = _dot(wbf, ef2_ref[0:LANES, :]).astype(BF16)
        wb_ref[...] = _dot(wbf, eb2_ref[0:LANES, :]).astype(BF16)
        prf = _expand2(pr, ef2_ref)
        prf_ref[...] = prf
        exf_ref[...] = jnp.exp(prf)
        prb = _expand2(pr, eb2_ref)
        prb_ref[...] = prb
        exb_ref[...] = jnp.exp(prb)
        sbc_ref[nc - 1] = ssd_sbin_ref[0, 0]

    def ssd_bwd(c):
        r0 = c * q
        rows = slice(r0, r0 + q)
        sb = [sbc_ref[c, g] for g in range(SSD_N_GROUPS)]
        new = _ssd_bwd_state_step(sb, xc_ref[rows, 0:SSD_D_INNER], xc_ref[rows, b_off:c_off],
                                  wb_ref[rows, :], exb_ref[r0:r0 + 1, :])
        for g in range(SSD_N_GROUPS):
            sbc_ref[c - 1, g] = new[g]

    ii = lax.broadcasted_iota(jnp.int32, (q, LANES), 0)
    ll = lax.broadcasted_iota(jnp.int32, (q, LANES), 1)
    jj = jnp.where(ll < SSD_HEAD_DIM, ll, ll - SSD_HEAD_DIM)
    lower = ii > jj
    upper = ii < jj
    left = ll < SSD_HEAD_DIM
    left_row = lax.broadcasted_iota(jnp.int32, (1, LANES), 1) < SSD_HEAD_DIM

    def ssd_chunk(c):
        k0 = (c // 2) * 2 * q
        s = c % 2
        r0 = c * q
        rows = slice(r0, r0 + q)
        row_t = rowarg_ref[k0:k0 + 2 * q, :].T
        row_tr = pltpu.roll(row_t, q, 1)
        lo_src, hi_src = (row_t, row_tr) if s == 0 else (row_tr, row_t)

        def pair_row(base, h0):
            return jnp.where(left_row, lo_src[base + h0:base + h0 + 1, :],
                             hi_src[base + h0 + 1:base + h0 + 2, :])

        for g in range(SSD_N_GROUPS):
            gl = slice(g * SSD_GROUP_W, (g + 1) * SSD_GROUP_W)
            b_g = xc_ref[rows, b_off + g * SSD_D_STATE:b_off + (g + 1) * SSD_D_STATE]
            c_g = xc_ref[rows, c_off + g * SSD_D_STATE:c_off + (g + 1) * SSD_D_STATE]
            cb2 = _dot_nt(c_g, jnp.concatenate([b_g, b_g], axis=0))
            sf = sf_ref[g]
            y_off = (_dot(c_g, sf.astype(BF16)) * exf_ref[rows, gl]
                     + _dot(c_g, sbc_ref[c, g].astype(BF16)) * exb_ref[rows, gl])
            xs_g = xc_ref[rows, gl].astype(F32)
            for pm in range(hpg // 2):
                h0 = g * hpg + 2 * pm
                cl = slice(g * SSD_GROUP_W + pm * LANES, g * SSD_GROUP_W + (pm + 1) * LANES)
                arg = jnp.where(
                    lower, prf_ref[rows, cl] + pair_row(0, h0),
                    jnp.where(upper, prb_ref[rows, cl] + pair_row(SSD_N_HEADS, h0),
                              pair_row(2 * SSD_N_HEADS, h0)))
                m2 = (cb2 * jnp.exp(arg)).astype(BF16)
                x_pair = xs_g[:, pm * LANES:(pm + 1) * LANES]
                x2 = jnp.concatenate([jnp.where(left, x_pair, 0.0), jnp.where(left, 0.0, x_pair)],
                                     axis=0).astype(BF16)
                y_ref[rows, cl] = (_dot(m2, x2) + y_off[:, pm * LANES:(pm + 1) * LANES]
                                   + drow_ref[:, cl] * x_pair)
            xd = xc_ref[rows, gl] * wf_ref[rows, gl]
            decf = exf_ref[r0 + q - 1:r0 + q, gl]
            sf_ref[g] = decf * sf + _dot_tn(b_g, xd)

    def ssd_epilogue():
        z = z_ref[...].astype(F32)
        y = y_ref[...] * _silu(z)
        outs = []
        for g in range(SSD_N_GROUPS):
            gl = slice(g * SSD_GROUP_W, (g + 1) * SSD_GROUP_W)
            yg = y[:, gl]
            ms = jnp.mean(yg * yg, axis=-1, keepdims=True)
            outs.append((yg * lax.rsqrt(ms + EPS) * snormw_ref[:, gl]).astype(BF16))
        branch_a = _dot(jnp.concatenate(outs, axis=1), swout_ref[...])
        return _sigmoid(gates_ref[:, 0:D_MODEL].astype(F32)) * branch_a

    def gla_prep():
        la = _gla_log_alpha(small_ref, w2_ref[...], gb_ref[...])
        cs_ref[:, 0:GLA_DK] = _dot_exact_lhs(tril_ref[...], la[:, 0:GLA_DK], 2)
        cs_ref[:, GLA_DK:2 * GLA_DK] = _dot_exact_lhs(triu_ref[...], la[:, GLA_DK:2 * GLA_DK], 2)
        gsbc_ref[nc - 1] = gla_sbin_ref[0, 0]

    def gla_bwd(c):
        rows = slice(c * q, (c + 1) * q)
        for h in range(GLA_N_HEADS):
            kl = slice(GLA_DK + h * GLA_HEAD_K, GLA_DK + (h + 1) * GLA_HEAD_K)
            k_h = qkvg_ref[rows, kl].astype(F32)
            v_h = qkvg_ref[rows, 2 * GLA_DK + h * GLA_HEAD_V:2 * GLA_DK + (h + 1) * GLA_HEAD_V]
            gsbc_ref[c - 1, h] = _gla_bwd_state_step(gsbc_ref[c, h], k_h, v_h, cs_ref[rows, kl])

    gi = lax.broadcasted_iota(jnp.int32, (q, q), 0)
    gj = lax.broadcasted_iota(jnp.int32, (q, q), 1)
    lower_inc = gi >= gj
    upper_inc = gi <= gj

    def gla_chunk(c):
        rows = slice(c * q, (c + 1) * q)
        for h in range(GLA_N_HEADS):
            fl = slice(h * GLA_HEAD_K, (h + 1) * GLA_HEAD_K)
            bl = slice(GLA_DK + h * GLA_HEAD_K, GLA_DK + (h + 1) * GLA_HEAD_K)
            q_h = qkvg_ref[rows, fl].astype(F32) * scale
            k_h = qkvg_ref[rows, bl].astype(F32)
            v_h = qkvg_ref[rows, 2 * GLA_DK + h * GLA_HEAD_V:2 * GLA_DK + (h + 1) * GLA_HEAD_V]
            csf = cs_ref[rows, fl]
            csb = cs_ref[rows, bl]
            qf = (q_h * jnp.exp(csf)).astype(BF16)
            kf = (k_h * jnp.exp(-csf)).astype(BF16)
            qb = (q_h * jnp.exp(csb)).astype(BF16)
            kb = (k_h * jnp.exp(-csb)).astype(BF16)
            att = (jnp.where(lower_inc, _dot_nt(qf, kf), 0.0)
                   + jnp.where(upper_inc, _dot_nt(qb, kb), 0.0))
            sf = gsf_ref[h]
            s_cat = jnp.concatenate([sf, gsbc_ref[c, h]], axis=1).astype(BF16)
            o = _dot(att.astype(BF16), v_h) + _dot_nt(jnp.concatenate([qf, qb], axis=1), s_cat)
            oacc_ref[rows, h * GLA_HEAD_V:(h + 1) * GLA_HEAD_V] = o
            tot = csf[q - 1:q, :]
            k_dec = (k_h * jnp.exp(tot - csf)).astype(BF16)
            gsf_ref[h] = jnp.exp(tot) * sf + _dot_tn(v_h, k_dec)

    def gla_epilogue(a_gated):
        g = qkvg_ref[:, 2 * GLA_DK + GLA_DV:2 * GLA_DK + 2 * GLA_DV].astype(F32)
        o = oacc_ref[...]
        outs = []
        for h in range(GLA_N_HEADS):
            hl = slice(h * GLA_HEAD_V, (h + 1) * GLA_HEAD_V)
            oh = o[:, hl]
            ms = jnp.mean(oh * oh, axis=-1, keepdims=True)
            outs.append((oh * lax.rsqrt(ms + EPS) * gnormw_ref[:, hl] * _silu(g[:, hl])).astype(BF16))
        branch_b = _dot(jnp.concatenate(outs, axis=1), gwout_ref[...])
        mixed = a_gated + _sigmoid(gates_ref[:, D_MODEL:2 * D_MODEL].astype(F32)) * branch_b
        o_ref[...] = x_ref[...] + _dot(mixed.astype(BF16), wo_ref[...])

    ssd_prep()
    gla_prep()
    for c in range(nc - 1, 0, -1):
        ssd_bwd(c)
        gla_bwd(c)
    for c in range(nc):
        ssd_chunk(c)
        gla_chunk(c)
    gla_epilogue(ssd_epilogue())


def _mix(big, small, ssd_sb, gla_sb, x, batch, seq_len, wts):
    nt = batch * seq_len
    tile = SSD_TILE
    n_tiles = seq_len // tile
    nc = tile // SSD_Q
    ssd_state = (SSD_N_GROUPS, SSD_D_STATE, SSD_GROUP_W)
    gla_state = (GLA_N_HEADS, GLA_HEAD_V, GLA_HEAD_K)
    consts = [wts["ssd_bias"], wts["ssd_nega"], wts["ssd_tri"], wts["ssd_ones"], wts["ssd_ef2"], wts["ssd_eb2"],
              wts["ssd_drow"], wts["ssd_normw"], wts["w_ssd_out"],
              wts["gla_w2"], wts["gla_gb"], wts["gla_tril"], wts["gla_triu"], wts["gla_normw"],
              wts["w_gla_out"], wts["w_o"]]
    qkvg_w = 2 * GLA_DK + 2 * GLA_DV
    z_block = (SSD_XBC + qkvg_w) // SSD_D_INNER
    gates_block = (BIG_W - 2 * D_MODEL) // (2 * D_MODEL)
    row = lambda b, t: (b * n_tiles + t, 0)
    wide = pltpu.VMEM((tile, SSD_D_INNER), F32)
    wide16 = pltpu.VMEM((tile, SSD_D_INNER), BF16)
    return pl.pallas_call(
        _mix_kernel,
        grid=(batch, n_tiles),
        in_specs=[pl.BlockSpec((tile, SSD_XBC), row),
                  pl.BlockSpec((tile, qkvg_w), lambda b, t: (b * n_tiles + t, SSD_XBC // qkvg_w)),
                  pl.BlockSpec((tile, SSD_D_INNER), lambda b, t: (b * n_tiles + t, z_block)),
                  pl.BlockSpec((tile, 2 * D_MODEL), lambda b, t: (b * n_tiles + t, gates_block)),
                  pl.BlockSpec((tile, SMALL_W), row),
                  pl.BlockSpec((1, 1) + ssd_state, lambda b, t: (b, t, 0, 0, 0)),
                  pl.BlockSpec((1, 1) + gla_state, lambda b, t: (b, t, 0, 0, 0)),
                  pl.BlockSpec((tile, D_MODEL), row)]
        + [_const_spec(c) for c in consts],
        out_specs=pl.BlockSpec((tile, D_MODEL), row),
        out_shape=jax.ShapeDtypeStruct((nt, D_MODEL), F32),
        scratch_shapes=[pltpu.VMEM((tile, SMALL_W), F32),
                        wide, wide, wide16, wide16, wide, wide,
                        pltpu.VMEM(ssd_state, F32),
                        pltpu.VMEM((nc,) + ssd_state, F32),
                        wide,
                        pltpu.VMEM((tile, 2 * GLA_DK), F32),
                        pltpu.VMEM(gla_state, F32),
                        pltpu.VMEM((nc,) + gla_state, F32),
                        pltpu.VMEM((tile, GLA_DV), F32)],
        compiler_params=_cparams(2),
        name="mix",
    )(big, big, big, big, small, ssd_sb, gla_sb, x, *consts)


def _ffn_kernel(prev_ref, cur_ref, next_ref, n2_ref, wup_ref, cw_ref, cb_ref, wd_ref, fn_ref, o_ref,
                h_ref, ug_ref, uv_ref, *, n_tiles):
    t = pl.program_id(1)
    tile = FFN_TILE

    def norm_rows(x):
        ms = jnp.mean(x * x, axis=-1, keepdims=True)
        return x * lax.rsqrt(ms + EPS) * n2_ref[...]

    h_ref[0:HALO, :] = norm_rows(prev_ref[...])
    h_ref[HALO:HALO + tile, :] = norm_rows(cur_ref[...])
    h_ref[HALO + tile:HALO + tile + HALO, :] = norm_rows(next_ref[...])
    h = h_ref[...].astype(BF16)

    row = lax.broadcasted_iota(jnp.int32, (tile + 2 * HALO, 1), 0)
    keep = jnp.logical_and(jnp.logical_or(row >= HALO, t > 0),
                           jnp.logical_or(row < HALO + tile, t < n_tiles - 1))

    def up(ci):
        lo, hi = FFN_CHUNKS[ci]
        ug_ref[ci % 2, :, 0:hi - lo] = jnp.where(keep, _dot(h, wup_ref[:, lo:hi]), 0.0)
        uv_ref[ci % 2, :, 0:hi - lo] = jnp.where(keep, _dot(h, wup_ref[:, D_FF + lo:D_FF + hi]), 0.0)

    def conv3(u_ref, ci, col0):
        lo, hi = FFN_CHUNKS[ci]
        acc = cb_ref[:, col0 + lo:col0 + hi]
        for k in range(3):
            u_k = u_ref[ci % 2, HALO - 1 + k:HALO - 1 + k + tile, 0:hi - lo]
            acc = acc + u_k * cw_ref[k:k + 1, col0 + lo:col0 + hi]
        return acc

    def down(ci):
        lo, hi = FFN_CHUNKS[ci]
        act = _silu(conv3(ug_ref, ci, 0)) * conv3(uv_ref, ci, D_FF)
        return _dot(act.astype(BF16), wd_ref[lo:hi, :])

    up(0)
    acc = None
    for ci in range(len(FFN_CHUNKS)):
        if ci + 1 < len(FFN_CHUNKS):
            up(ci + 1)
        d = down(ci)
        acc = d if acc is None else acc + d

    x2 = cur_ref[...] + acc
    ms = jnp.mean(x2 * x2, axis=-1, keepdims=True)
    o_ref[...] = x2 * lax.rsqrt(ms + EPS) * fn_ref[...]


def _ffn(x1, batch, seq_len, wts):
    nt = batch * seq_len
    tile = FFN_TILE
    n_tiles = seq_len // tile
    rb = tile // HALO
    seq_rb = seq_len // HALO
    last_rb = nt // HALO - 1
    c2 = lambda b, t: (0, 0)
    chunk_w = max(hi - lo for lo, hi in FFN_CHUNKS)
    resident = pl.Buffered(1)
    return pl.pallas_call(
        functools.partial(_ffn_kernel, n_tiles=n_tiles),
        grid=(batch, n_tiles),
        in_specs=[pl.BlockSpec((HALO, D_MODEL), lambda b, t: (jnp.maximum(b * seq_rb + t * rb - 1, 0), 0)),
                  pl.BlockSpec((tile, D_MODEL), lambda b, t: (b * n_tiles + t, 0)),
                  pl.BlockSpec((HALO, D_MODEL),
                               lambda b, t: (jnp.minimum(b * seq_rb + (t + 1) * rb, last_rb), 0)),
                  pl.BlockSpec((1, D_MODEL), c2),
                  pl.BlockSpec((D_MODEL, 2 * D_FF), c2, pipeline_mode=resident),
                  pl.BlockSpec((3, 2 * D_FF), c2),
                  pl.BlockSpec((1, 2 * D_FF), c2),
                  pl.BlockSpec((D_FF, D_MODEL), c2, pipeline_mode=resident),
                  pl.BlockSpec((1, D_MODEL), c2)],
        out_specs=pl.BlockSpec((tile, D_MODEL), lambda b, t: (b * n_tiles + t, 0)),
        out_shape=jax.ShapeDtypeStruct((nt, D_MODEL), F32),
        scratch_shapes=[pltpu.VMEM((tile + 2 * HALO, D_MODEL), F32),
                        pltpu.VMEM((2, tile + 2 * HALO, chunk_w), F32),
                        pltpu.VMEM((2, tile + 2 * HALO, chunk_w), F32)],
        compiler_params=_cparams(2),
        name="ffn",
    )(x1, x1, x1, wts["norm2_w"], wts["w_ffn_up"], wts["ffn_conv_w"], wts["ffn_conv_b"], wts["w_ffn_down"],
      wts["final_norm_w"])


def _block_tri(n, blk, kind):
    i = np.arange(n)[:, None]
    j = np.arange(n)[None, :]
    m = (i // blk) == (j // blk)
    if kind == "lower":
        m = m & (i >= j)
    elif kind == "upper":
        m = m & (i <= j)
    return jnp.asarray(m, dtype=BF16)


def _prep_weights(norm1_w, w_in, ssd_conv_w, ssd_conv_b, ssd_a_log, ssd_dt_bias, ssd_d, ssd_norm_w, w_ssd_out,
                  gla_gate_w2, gla_gate_b, gla_norm_w, w_gla_out, w_o, norm2_w, w_ffn_up, ffn_conv_w, ffn_conv_b,
                  w_ffn_down, final_norm_w):
    o_z = 0
    o_xbc = o_z + SSD_D_INNER
    o_dt = o_xbc + SSD_XBC
    o_q = o_dt + 2 * SSD_N_HEADS
    o_k = o_q + GLA_DK
    o_v = o_k + GLA_DK
    o_g = o_v + GLA_DV
    o_gk = o_g + GLA_DV
    o_gate = o_gk + 2 * GLA_GATE_RANK
    w_big = jnp.concatenate([w_in[:, o_xbc:o_dt], w_in[:, o_q:o_gk], w_in[:, o_z:o_xbc], w_in[:, o_gate:]],
                            axis=1).astype(BF16)
    pad = SMALL_W - 2 * SSD_N_HEADS - 2 * GLA_GATE_RANK
    w_small = jnp.concatenate([w_in[:, o_dt:o_q], w_in[:, o_gk:o_gate], jnp.zeros((D_MODEL, pad), F32)],
                              axis=1).astype(BF16)

    def row128(v):
        return jnp.concatenate([v.reshape(-1), jnp.zeros((SMALL_W - v.size,), F32)]).reshape(1, SMALL_W)

    head_of_lane = np.arange(SSD_D_INNER) // SSD_HEAD_DIM
    ef = (np.arange(SMALL_W)[:, None] == head_of_lane[None, :])
    eb = (np.arange(SMALL_W)[:, None] == (head_of_lane[None, :] + SSD_N_HEADS))
    w2 = jnp.zeros((SMALL_W, 2 * GLA_DK), F32)
    w2 = w2.at[GK_OFF:GK_OFF + GLA_GATE_RANK, :GLA_DK].set(gla_gate_w2[0])
    w2 = w2.at[GK_OFF + GLA_GATE_RANK:GK_OFF + 2 * GLA_GATE_RANK, GLA_DK:].set(gla_gate_w2[1])
    return {
        "norm1_w": norm1_w.reshape(1, D_MODEL),
        "w_big": w_big,
        "w_small": w_small,
        "ssd_conv_w": ssd_conv_w,
        "ssd_conv_b": ssd_conv_b.reshape(1, SSD_XBC),
        "ssd_bias": row128(ssd_dt_bias),
        "ssd_nega": row128(-jnp.exp(ssd_a_log)),
        "ssd_tri": _block_tri(SSD_TILE, SSD_Q, "lower"),
        "ssd_ones": _block_tri(SSD_TILE, SSD_Q, "ones"),
        "ssd_tri_full": _block_tri(PRE_TILE, PRE_TILE, "lower"),
        "ssd_ones_full": _block_tri(PRE_TILE, PRE_TILE, "ones"),
        "ssd_ef2": jnp.asarray(np.concatenate([ef, ef], axis=0), dtype=BF16),
        "ssd_eb2": jnp.asarray(np.concatenate([eb, eb], axis=0), dtype=BF16),
        "ssd_drow": jnp.repeat(ssd_d, SSD_HEAD_DIM).reshape(1, SSD_D_INNER),
        "ssd_normw": ssd_norm_w.reshape(1, SSD_D_INNER),
        "w_ssd_out": w_ssd_out.astype(BF16),
        "gla_w2": w2.astype(BF16),
        "gla_gb": gla_gate_b.reshape(1, 2 * GLA_DK),
        "gla_tril": _block_tri(GLA_TILE, GLA_CHUNK, "lower"),
        "gla_triu": _block_tri(GLA_TILE, GLA_CHUNK, "upper"),
        "gla_triu_full": _block_tri(PRE_TILE, PRE_TILE, "upper"),
        "gla_normw": gla_norm_w.reshape(1, GLA_DV),
        "w_gla_out": w_gla_out.astype(BF16),
        "w_o": w_o.astype(BF16),
        "norm2_w": norm2_w.reshape(1, D_MODEL),
        "w_ffn_up": w_ffn_up.astype(BF16),
        "ffn_conv_w": ffn_conv_w,
        "ffn_conv_b": ffn_conv_b.reshape(1, 2 * D_FF),
        "w_ffn_down": w_ffn_down.astype(BF16),
        "final_norm_w": final_norm_w.reshape(1, D_MODEL),
    }


def _trunk(x3, wts):
    batch, seq_len, _ = x3.shape
    x = x3.reshape(batch * seq_len, D_MODEL)
    big, small, ssd_sb, gla_sb = _proj_pre(x, batch, seq_len, wts)
    x1 = _mix(big, small, ssd_sb, gla_sb, x, batch, seq_len, wts)
    out = _ffn(x1, batch, seq_len, wts)
    return out.reshape(batch, seq_len, D_MODEL)


def kernel(x_prompt, x_sample, norm1_w, w_in, ssd_conv_w, ssd_conv_b, ssd_a_log, ssd_dt_bias, ssd_d, ssd_norm_w,
           w_ssd_out, gla_gate_w2, gla_gate_b, gla_norm_w, w_gla_out, w_o, norm2_w, w_ffn_up, ffn_conv_w,
           ffn_conv_b, w_ffn_down, final_norm_w):
    wts = _prep_weights(norm1_w[0], w_in[0], ssd_conv_w[0], ssd_conv_b[0], ssd_a_log[0], ssd_dt_bias[0], ssd_d[0],
                        ssd_norm_w[0], w_ssd_out[0], gla_gate_w2[0], gla_gate_b[0], gla_norm_w[0], w_gla_out[0],
                        w_o[0], norm2_w[0], w_ffn_up[0], ffn_conv_w[0], ffn_conv_b[0], w_ffn_down[0], final_norm_w)
    return (_trunk(x_prompt, wts), _trunk(x_sample, wts))
```

```python
import functools

import numpy as np
import jax
import jax.numpy as jnp
from jax import lax
from jax.experimental import pallas as pl
from jax.experimental.pallas import tpu as pltpu

F32 = jnp.float32
BF16 = jnp.bfloat16

D_MODEL = 1024
SSD_D_INNER = 2048
SSD_HEAD_DIM = 64
SSD_N_HEADS = 32
SSD_N_GROUPS = 4
SSD_D_STATE = 128
SSD_GROUP_W = SSD_D_INNER // SSD_N_GROUPS
SSD_BC_W = SSD_N_GROUPS * SSD_D_STATE
SSD_XBC = SSD_D_INNER + 2 * SSD_BC_W
SSD_CONV = 5
GLA_N_HEADS = 4
GLA_DK = 512
GLA_DV = 1024
GLA_HEAD_K = 128
GLA_HEAD_V = 256
GLA_GATE_RANK = 16
GLA_GATE_NORM = 16.0
GLA_CHUNK = 64
D_FF = 2816
EPS = 1e-6

LANES = 128
SUBLANES = 8
VMEM_LIMIT = 56 * 1024 * 1024

BIG_W = SSD_XBC + 2 * GLA_DK + 2 * GLA_DV + SSD_D_INNER + 2 * D_MODEL
SMALL_W = LANES
GK_OFF = 2 * SSD_N_HEADS

PRE_TILE = 256
SSD_TILE = 256
SSD_Q = 64
assert PRE_TILE == SSD_TILE
assert SSD_Q == GLA_CHUNK
HALO = SUBLANES
SSD_HALO = 2 * SUBLANES
CONV_SLAB = 256
FFN_TILE = 512
FFN_CHUNKS = ((0, 1024), (1024, 2048), (2048, D_FF))
REST_TN = 1024


def _dot(a, b):
    return jnp.dot(a, b, preferred_element_type=F32)


def _dot_tn(a, b):
    return lax.dot_general(a, b, (((0,), (0,)), ((), ())), preferred_element_type=F32)


def _dot_nt(a, b):
    return lax.dot_general(a, b, (((1,), (1,)), ((), ())), preferred_element_type=F32)


def _split_bf16(x, n):
    parts = []
    r = x
    for i in range(n):
        p = r.astype(BF16)
        parts.append(p)
        if i + 1 < n:
            r = r - p.astype(F32)
    return parts


def _dot_exact_lhs(m_bf16, x, n=3):
    acc = None
    for p in _split_bf16(x, n):
        t = _dot(m_bf16, p)
        acc = t if acc is None else acc + t
    return acc


def _sigmoid(x):
    return 0.5 * jnp.tanh(0.5 * x) + 0.5


def _silu(x):
    return x * _sigmoid(x)


def _softplus(x):
    return jnp.maximum(x, 0.0) + jnp.log1p(jnp.exp(-jnp.abs(x)))


def _cparams(n_axes):
    return pltpu.CompilerParams(dimension_semantics=("arbitrary",) * n_axes,
                                vmem_limit_bytes=VMEM_LIMIT)


def _proj_pre_kernel(xprev_ref, xcur_ref, xnext_ref, nw_ref, wbig_ref, wsmall_ref,
                     convw_ref, convb_ref, bias_ref, nega_ref, tri_ref, ones_ref, eb2_ref,
                     w2_ref, gb_ref, triu_ref,
                     big_ref, small_ref, ssd_sbin_ref, gla_sbin_ref,
                     h_ref, ext_ref, ssd_sb_ref, gla_sb_ref, *, n_tiles):
    t = pl.program_id(1)
    tile = PRE_TILE
    halo = SSD_HALO
    first = t == n_tiles - 1
    last = t == 0

    @pl.when(t == 0)
    def _():
        ssd_sb_ref[...] = jnp.zeros_like(ssd_sb_ref)
        gla_sb_ref[...] = jnp.zeros_like(gla_sb_ref)

    ssd_sbin_ref[0, 0] = ssd_sb_ref[...]
    gla_sbin_ref[0, 0] = gla_sb_ref[...]

    def norm_rows(x):
        ms = jnp.mean(x * x, axis=-1, keepdims=True)
        return (x * lax.rsqrt(ms + EPS) * nw_ref[...]).astype(BF16)

    h_ref[0:halo, :] = norm_rows(xprev_ref[...])
    h_ref[halo:halo + tile, :] = norm_rows(xcur_ref[...])
    h_ref[halo + tile:halo + tile + halo, :] = norm_rows(xnext_ref[...])

    row = lax.broadcasted_iota(jnp.int32, (tile + 2 * halo, 1), 0)
    keep = jnp.logical_and(jnp.logical_or(row >= halo, jnp.logical_not(first)),
                           jnp.logical_or(row < halo + tile, jnp.logical_not(last)))
    h = h_ref[halo:halo + tile, :]

    def xbc_slab(lo):
        ext_ref[(lo // CONV_SLAB) % 2] = jnp.where(keep, _dot(h_ref[...], wbig_ref[:, lo:lo + CONV_SLAB]), 0.0)

    def conv_slab(lo):
        ext = ext_ref.at[(lo // CONV_SLAB) % 2]
        y = _conv_taps(ext, convw_ref, convb_ref, halo, tile, lo, lo + CONV_SLAB)
        big_ref[:, lo:lo + CONV_SLAB] = _silu(y).astype(BF16)

    def rest_chunk(lo):
        big_ref[:, lo:lo + REST_TN] = _dot(h, wbig_ref[:, lo:lo + REST_TN]).astype(BF16)

    slabs = list(range(0, SSD_XBC, CONV_SLAB))
    rest = list(range(SSD_XBC, BIG_W, REST_TN))
    xbc_slab(slabs[0])
    for i, lo in enumerate(slabs):
        if i + 1 < len(slabs):
            xbc_slab(slabs[i + 1])
        if rest:
            rest_chunk(rest.pop(0))
        conv_slab(lo)
    for lo in rest:
        rest_chunk(lo)
    small_ref[...] = _dot(h, wsmall_ref[...])

    pr, w, _ = _ssd_small(small_ref, bias_ref, nega_ref, tri_ref, ones_ref)
    wb = _dot(w.astype(BF16), eb2_ref[0:LANES, :]).astype(BF16)
    decb = _expand2(jnp.exp(pr[0:SUBLANES, :]), eb2_ref)[0:1, :]
    sb = [ssd_sb_ref[g] for g in range(SSD_N_GROUPS)]
    new = _ssd_bwd_state_step(sb, big_ref[:, 0:SSD_D_INNER],
                              big_ref[:, SSD_D_INNER:SSD_D_INNER + SSD_BC_W], wb, decb)
    for g in range(SSD_N_GROUPS):
        ssd_sb_ref[g] = new[g]

    la = _gla_log_alpha(small_ref, w2_ref[:, GLA_DK:], gb_ref[:, GLA_DK:])
    cs = _dot_exact_lhs(triu_ref[...], la, 2)
    k_off = SSD_XBC + GLA_DK
    v_off = SSD_XBC + 2 * GLA_DK
    for hd in range(GLA_N_HEADS):
        k_h = big_ref[:, k_off + hd * GLA_HEAD_K:k_off + (hd + 1) * GLA_HEAD_K].astype(F32)
        v_h = big_ref[:, v_off + hd * GLA_HEAD_V:v_off + (hd + 1) * GLA_HEAD_V]
        gla_sb_ref[hd] = _gla_bwd_state_step(gla_sb_ref[hd], k_h, v_h,
                                             cs[:, hd * GLA_HEAD_K:(hd + 1) * GLA_HEAD_K])


def _proj_pre(x, batch, seq_len, wts):
    nt = batch * seq_len
    tile = PRE_TILE
    n_tiles = seq_len // tile
    rb = tile // SSD_HALO
    seq_rb = seq_len // SSD_HALO
    last_rb = nt // SSD_HALO - 1
    rev = lambda t: n_tiles - 1 - t
    ssd_state = (SSD_N_GROUPS, SSD_D_STATE, SSD_GROUP_W)
    gla_state = (GLA_N_HEADS, GLA_HEAD_V, GLA_HEAD_K)
    consts = [wts["ssd_conv_w"], wts["ssd_conv_b"], wts["ssd_bias"], wts["ssd_nega"],
              wts["ssd_tri_full"], wts["ssd_ones_full"], wts["ssd_eb2"],
              wts["gla_w2"], wts["gla_gb"], wts["gla_triu_full"]]
    row = lambda b, t: (b * n_tiles + rev(t), 0)
    return pl.pallas_call(
        functools.partial(_proj_pre_kernel, n_tiles=n_tiles),
        grid=(batch, n_tiles),
        in_specs=[pl.BlockSpec((SSD_HALO, D_MODEL),
                               lambda b, t: (jnp.maximum(b * seq_rb + rev(t) * rb - 1, 0), 0)),
                  pl.BlockSpec((tile, D_MODEL), row),
                  pl.BlockSpec((SSD_HALO, D_MODEL),
                               lambda b, t: (jnp.minimum(b * seq_rb + (rev(t) + 1) * rb, last_rb), 0)),
                  pl.BlockSpec((1, D_MODEL), lambda b, t: (0, 0)),
                  pl.BlockSpec((D_MODEL, BIG_W), lambda b, t: (0, 0), pipeline_mode=pl.Buffered(1)),
                  pl.BlockSpec((D_MODEL, SMALL_W), lambda b, t: (0, 0))]
        + [_const_spec(c) for c in consts],
        out_specs=[pl.BlockSpec((tile, BIG_W), row),
                   pl.BlockSpec((tile, SMALL_W), row),
                   pl.BlockSpec((1, 1) + ssd_state, lambda b, t: (b, rev(t), 0, 0, 0)),
                   pl.BlockSpec((1, 1) + gla_state, lambda b, t: (b, rev(t), 0, 0, 0))],
        out_shape=[jax.ShapeDtypeStruct((nt, BIG_W), BF16),
                   jax.ShapeDtypeStruct((nt, SMALL_W), F32),
                   jax.ShapeDtypeStruct((batch, n_tiles) + ssd_state, F32),
                   jax.ShapeDtypeStruct((batch, n_tiles) + gla_state, F32)],
        scratch_shapes=[pltpu.VMEM((tile + 2 * SSD_HALO, D_MODEL), BF16),
                        pltpu.VMEM((2, tile + 2 * SSD_HALO, CONV_SLAB), F32),
                        pltpu.VMEM(ssd_state, F32),
                        pltpu.VMEM(gla_state, F32)],
        compiler_params=_cparams(2),
        name="proj_pre",
    )(x, x, x, wts["norm1_w"], wts["w_big"], wts["w_small"], *consts)


def _conv_taps(ext_ref, w_ref, b_ref, row0, n_rows, lo, hi):
    pad = (SSD_CONV - 1) // 2
    acc = b_ref[:, lo:hi]
    for k in range(SSD_CONV):
        acc = acc + ext_ref[row0 - pad + k:row0 - pad + k + n_rows, :] * w_ref[k:k + 1, lo:hi]
    return acc


def _ssd_small(small_ref, bias_ref, nega_ref, tri_ref, ones_ref):
    s = small_ref[...]
    dt = _softplus(s + bias_ref[...])
    a = dt * nega_ref[...]
    p = _dot_exact_lhs(tri_ref[...], a)
    tot = _dot_exact_lhs(ones_ref[...], a)
    col = lax.broadcasted_iota(jnp.int32, s.shape, 1)
    fwd = col < SSD_N_HEADS
    r = tot - p + a
    pr = jnp.where(fwd, p, r)
    w = dt * jnp.exp(jnp.where(fwd, tot - p, p - a))
    rowarg = jnp.log(dt) - pr
    dsum = pltpu.roll(dt, 2 * SSD_N_HEADS, 1) + pltpu.roll(dt, SSD_N_HEADS, 1)
    diag_cols = (col >= 2 * SSD_N_HEADS) & (col < 3 * SSD_N_HEADS)
    rowarg = jnp.where(diag_cols, jnp.log(dsum), rowarg)
    return pr, w, rowarg


def _expand2(x, e2_ref):
    hi, lo = _split_bf16(x, 2)
    return _dot(jnp.concatenate([hi, lo], axis=1), e2_ref[...])


def _ssd_bwd_state_step(sb, xs_c, b_c, wb_c, decb_row):
    out = []
    for g in range(SSD_N_GROUPS):
        gl = slice(g * SSD_GROUP_W, (g + 1) * SSD_GROUP_W)
        xd = xs_c[:, gl] * wb_c[:, gl]
        bg = b_c[:, g * SSD_D_STATE:(g + 1) * SSD_D_STATE]
        out.append(decb_row[:, gl] * sb[g] + _dot_tn(bg, xd))
    return out


def _const_spec(arr):
    nd = arr.ndim
    return pl.BlockSpec(arr.shape, lambda b, t: (0,) * nd)


def _gla_log_alpha(small_ref, w2, gb):
    pre = _dot(small_ref[...].astype(BF16), w2) + gb
    return -(jnp.maximum(-pre, 0.0) + jnp.log(1.0 + jnp.exp(-jnp.abs(pre)))) * (1.0 / GLA_GATE_NORM)


def _gla_bwd_state_step(sb_h, k_h, v_h, cs_h):
    tot = cs_h[0:1, :]
    k_dec = (k_h * jnp.exp(tot - cs_h)).astype(BF16)
    return jnp.exp(tot) * sb_h + _dot_tn(v_h, k_dec)


def _mix_kernel(xc_ref, qkvg_ref, z_ref, gates_ref, small_ref, ssd_sbin_ref, gla_sbin_ref, x_ref,
                bias_ref, nega_ref, tri_ref, ones_ref, ef2_ref, eb2_ref, drow_ref, snormw_ref, swout_ref,
                w2_ref, gb_ref, tril_ref, triu_ref, gnormw_ref, gwout_ref, wo_ref,
                o_ref,
                rowarg_ref, prf_ref, prb_ref, wf_ref, wb_ref, exf_ref, exb_ref, sf_ref, sbc_ref, y_ref,
                cs_ref, gsf_ref, gsbc_ref, oacc_ref):
    t = pl.program_id(1)
    tile = SSD_TILE
    q = SSD_Q
    nc = tile // q
    b_off = SSD_D_INNER
    c_off = SSD_D_INNER + SSD_BC_W
    scale = GLA_HEAD_K ** -0.5
    hpg = SSD_N_HEADS // SSD_N_GROUPS

    @pl.when(t == 0)
    def _():
        sf_ref[...] = jnp.zeros_like(sf_ref)
        gsf_ref[...] = jnp.zeros_like(gsf_ref)

    def ssd_prep():
        pr, w, rowarg = _ssd_small(small_ref, bias_ref, nega_ref, tri_ref, ones_ref)
        rowarg_ref[...] = rowarg
        wbf = w.astype(BF16)
        wf_ref[...] = _dot(wbf, ef2_ref[0:LANES, :]).astype(BF16)
        wb_ref[...] = _dot(wbf, eb2_ref[0:LANES, :]).astype(BF16)
        prf = _expand2(pr, ef2_ref)
        prf_ref[...] = prf
        exf_ref[...] = jnp.exp(prf)
        prb = _expand2(pr, eb2_ref)
        prb_ref[...] = prb
        exb_ref[...] = jnp.exp(prb)
        sbc_ref[nc - 1] = ssd_sbin_ref[0, 0]

    def ssd_bwd(c):
        r0 = c * q
        rows = slice(r0, r0 + q)
        sb = [sbc_ref[c, g] for g in range(SSD_N_GROUPS)]
        new = _ssd_bwd_state_step(sb, xc_ref[rows, 0:SSD_D_INNER], xc_ref[rows, b_off:c_off],
                                  wb_ref[rows, :], exb_ref[r0:r0 + 1, :])
        for g in range(SSD_N_GROUPS):
            sbc_ref[c - 1, g] = new[g]

    ii = lax.broadcasted_iota(jnp.int32, (q, LANES), 0)
    ll = lax.broadcasted_iota(jnp.int32, (q, LANES), 1)
    jj = jnp.where(ll < SSD_HEAD_DIM, ll, ll - SSD_HEAD_DIM)
    lower = ii > jj
    upper = ii < jj
    left = ll < SSD_HEAD_DIM
    left_row = lax.broadcasted_iota(jnp.int32, (1, LANES), 1) < SSD_HEAD_DIM

    def ssd_chunk(c):
        k0 = (c // 2) * 2 * q
        s = c % 2
        r0 = c * q
        rows = slice(r0, r0 + q)
        row_t = rowarg_ref[k0:k0 + 2 * q, :].T
        row_tr = pltpu.roll(row_t, q, 1)
        lo_src, hi_src = (row_t, row_tr) if s == 0 else (row_tr, row_t)

        def pair_row(base, h0):
            return jnp.where(left_row, lo_src[base + h0:base + h0 + 1, :],
                             hi_src[base + h0 + 1:base + h0 + 2, :])

        for g in range(SSD_N_GROUPS):
            gl = slice(g * SSD_GROUP_W, (g + 1) * SSD_GROUP_W)
            b_g = xc_ref[rows, b_off + g * SSD_D_STATE:b_off + (g + 1) * SSD_D_STATE]
            c_g = xc_ref[rows, c_off + g * SSD_D_STATE:c_off + (g + 1) * SSD_D_STATE]
            cb2 = _dot_nt(c_g, jnp.concatenate([b_g, b_g], axis=0))
            sf = sf_ref[g]
            y_off = (_dot(c_g, sf.astype(BF16)) * exf_ref[rows, gl]
                     + _dot(c_g, sbc_ref[c, g].astype(BF16)) * exb_ref[rows, gl])
            xs_g = xc_ref[rows, gl].astype(F32)
            for pm in range(hpg // 2):
                h0 = g * hpg + 2 * pm
                cl = slice(g * SSD_GROUP_W + pm * LANES, g * SSD_GROUP_W + (pm + 1) * LANES)
                arg = jnp.where(
                    lower, prf_ref[rows, cl] + pair_row(0, h0),
                    jnp.where(upper, prb_ref[rows, cl] + pair_row(SSD_N_HEADS, h0),
                              pair_row(2 * SSD_N_HEADS, h0)))
                m2 = (cb2 * jnp.exp(arg)).astype(BF16)
                x_pair = xs_g[:, pm * LANES:(pm + 1) * LANES]
                x2 = jnp.concatenate([jnp.where(left, x_pair, 0.0), jnp.where(left, 0.0, x_pair)],
                                     axis=0).astype(BF16)
                y_ref[rows, cl] = (_dot(m2, x2) + y_off[:, pm * LANES:(pm + 1) * LANES]
                                   + drow_ref[:, cl] * x_pair)
            xd = xc_ref[rows, gl] * wf_ref[rows, gl]
            decf = exf_ref[r0 + q - 1:r0 + q, gl]
            sf_ref[g] = decf * sf + _dot_tn(b_g, xd)

    def ssd_epilogue():
        z = z_ref[...].astype(F32)
        y = y_ref[...] * _silu(z)
        outs = []
        for g in range(SSD_N_GROUPS):
            gl = slice(g * SSD_GROUP_W, (g + 1) * SSD_GROUP_W)
            yg = y[:, gl]
            ms = jnp.mean(yg * yg, axis=-1, keepdims=True)
            outs.append((yg * lax.rsqrt(ms + EPS) * snormw_ref[:, gl]).astype(BF16))
        branch_a = _dot(jnp.concatenate(outs, axis=1), swout_ref[...])
        return _sigmoid(gates_ref[:, 0:D_MODEL].astype(F32)) * branch_a

    def gla_prep():
        la = _gla_log_alpha(small_ref, w2_ref[...], gb_ref[...])
        cs_ref[:, 0:GLA_DK] = _dot_exact_lhs(tril_ref[...], la[:, 0:GLA_DK], 2)
        cs_ref[:, GLA_DK:2 * GLA_DK] = _dot_exact_lhs(triu_ref[...], la[:, GLA_DK:2 * GLA_DK], 2)
        gsbc_ref[nc - 1] = gla_sbin_ref[0, 0]

    def gla_bwd(c):
        rows = slice(c * q, (c + 1) * q)
        for h in range(GLA_N_HEADS):
            kl = slice(GLA_DK + h * GLA_HEAD_K, GLA_DK + (h + 1) * GLA_HEAD_K)
            k_h = qkvg_ref[rows, kl].astype(F32)
            v_h = qkvg_ref[rows, 2 * GLA_DK + h * GLA_HEAD_V:2 * GLA_DK + (h + 1) * GLA_HEAD_V]
            gsbc_ref[c - 1, h] = _gla_bwd_state_step(gsbc_ref[c, h], k_h, v_h, cs_ref[rows, kl])

    gi = lax.broadcasted_iota(jnp.int32, (q, q), 0)
    gj = lax.broadcasted_iota(jnp.int32, (q, q), 1)
    lower_inc = gi >= gj
    upper_inc = gi <= gj

    def gla_chunk(c):
        rows = slice(c * q, (c + 1) * q)
        for h in range(GLA_N_HEADS):
            fl = slice(h * GLA_HEAD_K, (h + 1) * GLA_HEAD_K)
            bl = slice(GLA_DK + h * GLA_HEAD_K, GLA_DK + (h + 1) * GLA_HEAD_K)
            q_h = qkvg_ref[rows, fl].astype(F32) * scale
            k_h = qkvg_ref[rows, bl].astype(F32)
            v_h = qkvg_ref[rows, 2 * GLA_DK + h * GLA_HEAD_V:2 * GLA_DK + (h + 1) * GLA_HEAD_V]
            csf = cs_ref[rows, fl]
            csb = cs_ref[rows, bl]
            qf = (q_h * jnp.exp(csf)).astype(BF16)
            kf = (k_h * jnp.exp(-csf)).astype(BF16)
            qb = (q_h * jnp.exp(csb)).astype(BF16)
            kb = (k_h * jnp.exp(-csb)).astype(BF16)
            att = (jnp.where(lower_inc, _dot_nt(qf, kf), 0.0)
                   + jnp.where(upper_inc, _dot_nt(qb, kb), 0.0))
            sf = gsf_ref[h]
            s_cat = jnp.concatenate([sf, gsbc_ref[c, h]], axis=1).astype(BF16)
            o = _dot(att.astype(BF16), v_h) + _dot_nt(jnp.concatenate([qf, qb], axis=1), s_cat)
            oacc_ref[rows, h * GLA_HEAD_V:(h + 1) * GLA_HEAD_V] = o
            tot = csf[q - 1:q, :]
            k_dec = (k_h * jnp.exp(tot - csf)).astype(BF16)
            gsf_ref[h] = jnp.exp(tot) * sf + _dot_tn(v_h, k_dec)

    def gla_epilogue(a_gated):
        g = qkvg_ref[:, 2 * GLA_DK + GLA_DV:2 * GLA_DK + 2 * GLA_DV].astype(F32)
        o = oacc_ref[...]
        outs = []
        for h in range(GLA_N_HEADS):
            hl = slice(h * GLA_HEAD_V, (h + 1) * GLA_HEAD_V)
            oh = o[:, hl]
            ms = jnp.mean(oh * oh, axis=-1, keepdims=True)
            outs.append((oh * lax.rsqrt(ms + EPS) * gnormw_ref[:, hl] * _silu(g[:, hl])).astype(BF16))
        branch_b = _dot(jnp.concatenate(outs, axis=1), gwout_ref[...])
        mixed = a_gated + _sigmoid(gates_ref[:, D_MODEL:2 * D_MODEL].astype(F32)) * branch_b
        o_ref[...] = x_ref[...] + _dot(mixed.astype(BF16), wo_ref[...])

    ssd_prep()
    gla_prep()
    for c in range(nc - 1, 0, -1):
        ssd_bwd(c)
        gla_bwd(c)
    for c in range(nc):
        ssd_chunk(c)
        gla_chunk(c)
    gla_epilogue(ssd_epilogue())


def _mix(big, small, ssd_sb, gla_sb, x, batch, seq_len, wts):
    nt = batch * seq_len
    tile = SSD_TILE
    n_tiles = seq_len // tile
    nc = tile // SSD_Q
    ssd_state = (SSD_N_GROUPS, SSD_D_STATE, SSD_GROUP_W)
    gla_state = (GLA_N_HEADS, GLA_HEAD_V, GLA_HEAD_K)
    consts = [wts["ssd_bias"], wts["ssd_nega"], wts["ssd_tri"], wts["ssd_ones"], wts["ssd_ef2"], wts["ssd_eb2"],
              wts["ssd_drow"], wts["ssd_normw"], wts["w_ssd_out"],
              wts["gla_w2"], wts["gla_gb"], wts["gla_tril"], wts["gla_triu"], wts["gla_normw"],
              wts["w_gla_out"], wts["w_o"]]
    qkvg_w = 2 * GLA_DK + 2 * GLA_DV
    z_block = (SSD_XBC + qkvg_w) // SSD_D_INNER
    gates_block = (BIG_W - 2 * D_MODEL) // (2 * D_MODEL)
    row = lambda b, t: (b * n_tiles + t, 0)
    wide = pltpu.VMEM((tile, SSD_D_INNER), F32)
    wide16 = pltpu.VMEM((tile, SSD_D_INNER), BF16)
    return pl.pallas_call(
        _mix_kernel,
        grid=(batch, n_tiles),
        in_specs=[pl.BlockSpec((tile, SSD_XBC), row),
                  pl.BlockSpec((tile, qkvg_w), lambda b, t: (b * n_tiles + t, SSD_XBC // qkvg_w)),
                  pl.BlockSpec((tile, SSD_D_INNER), lambda b, t: (b * n_tiles + t, z_block)),
                  pl.BlockSpec((tile, 2 * D_MODEL), lambda b, t: (b * n_tiles + t, gates_block)),
                  pl.BlockSpec((tile, SMALL_W), row),
                  pl.BlockSpec((1, 1) + ssd_state, lambda b, t: (b, t, 0, 0, 0)),
                  pl.BlockSpec((1, 1) + gla_state, lambda b, t: (b, t, 0, 0, 0)),
                  pl.BlockSpec((tile, D_MODEL), row)]
        + [_const_spec(c) for c in consts],
        out_specs=pl.BlockSpec((tile, D_MODEL), row),
        out_shape=jax.ShapeDtypeStruct((nt, D_MODEL), F32),
        scratch_shapes=[pltpu.VMEM((tile, SMALL_W), F32),
                        wide, wide, wide16, wide16, wide, wide,
                        pltpu.VMEM(ssd_state, F32),
                        pltpu.VMEM((nc,) + ssd_state, F32),
                        wide,
                        pltpu.VMEM((tile, 2 * GLA_DK), F32),
                        pltpu.VMEM(gla_state, F32),
                        pltpu.VMEM((nc,) + gla_state, F32),
                        pltpu.VMEM((tile, GLA_DV), F32)],
        compiler_params=_cparams(2),
        name="mix",
    )(big, big, big, big, small, ssd_sb, gla_sb, x, *consts)


def _ffn_kernel(prev_ref, cur_ref, next_ref, n2_ref, wup_ref, cw_ref, cb_ref, wd_ref, fn_ref, o_ref,
                h_ref, ug_ref, uv_ref, *, n_tiles):
    t = pl.program_id(1)
    tile = FFN_TILE

    def norm_rows(x):
        ms = jnp.mean(x * x, axis=-1, keepdims=True)
        return x * lax.rsqrt(ms + EPS) * n2_ref[...]

    h_ref[0:HALO, :] = norm_rows(prev_ref[...])
    h_ref[HALO:HALO + tile, :] = norm_rows(cur_ref[...])
    h_ref[HALO + tile:HALO + tile + HALO, :] = norm_rows(next_ref[...])
    h = h_ref[...].astype(BF16)

    row = lax.broadcasted_iota(jnp.int32, (tile + 2 * HALO, 1), 0)
    keep = jnp.logical_and(jnp.logical_or(row >= HALO, t > 0),
                           jnp.logical_or(row < HALO + tile, t < n_tiles - 1))

    def up(ci):
        lo, hi = FFN_CHUNKS[ci]
        ug_ref[ci % 2, :, 0:hi - lo] = jnp.where(keep, _dot(h, wup_ref[:, lo:hi]), 0.0)
        uv_ref[ci % 2, :, 0:hi - lo] = jnp.where(keep, _dot(h, wup_ref[:, D_FF + lo:D_FF + hi]), 0.0)

    def conv3(u_ref, ci, col0):
        lo, hi = FFN_CHUNKS[ci]
        acc = cb_ref[:, col0 + lo:col0 + hi]
        for k in range(3):
            u_k = u_ref[ci % 2, HALO - 1 + k:HALO - 1 + k + tile, 0:hi - lo]
            acc = acc + u_k * cw_ref[k:k + 1, col0 + lo:col0 + hi]
        return acc

    def down(ci):
        lo, hi = FFN_CHUNKS[ci]
        act = _silu(conv3(ug_ref, ci, 0)) * conv3(uv_ref, ci, D_FF)
        return _dot(act.astype(BF16), wd_ref[lo:hi, :])

    up(0)
    acc = None
    for ci in range(len(FFN_CHUNKS)):
        if ci + 1 < len(FFN_CHUNKS):
            up(ci + 1)
        d = down(ci)
        acc = d if acc is None else acc + d

    x2 = cur_ref[...] + acc
    ms = jnp.mean(x2 * x2, axis=-1, keepdims=True)
    o_ref[...] = x2 * lax.rsqrt(ms + EPS) * fn_ref[...]


def _ffn(x1, batch, seq_len, wts):
    nt = batch * seq_len
    tile = FFN_TILE
    n_tiles = seq_len // tile
    rb = tile // HALO
    seq_rb = seq_len // HALO
    last_rb = nt // HALO - 1
    c2 = lambda b, t: (0, 0)
    chunk_w = max(hi - lo for lo, hi in FFN_CHUNKS)
    resident = pl.Buffered(1)
    return pl.pallas_call(
        functools.partial(_ffn_kernel, n_tiles=n_tiles),
        grid=(batch, n_tiles),
        in_specs=[pl.BlockSpec((HALO, D_MODEL), lambda b, t: (jnp.maximum(b * seq_rb + t * rb - 1, 0), 0)),
                  pl.BlockSpec((tile, D_MODEL), lambda b, t: (b * n_tiles + t, 0)),
                  pl.BlockSpec((HALO, D_MODEL),
                               lambda b, t: (jnp.minimum(b * seq_rb + (t + 1) * rb, last_rb), 0)),
                  pl.BlockSpec((1, D_MODEL), c2),
                  pl.BlockSpec((D_MODEL, 2 * D_FF), c2, pipeline_mode=resident),
                  pl.BlockSpec((3, 2 * D_FF), c2),
                  pl.BlockSpec((1, 2 * D_FF), c2),
                  pl.BlockSpec((D_FF, D_MODEL), c2, pipeline_mode=resident),
                  pl.BlockSpec((1, D_MODEL), c2)],
        out_specs=pl.BlockSpec((tile, D_MODEL), lambda b, t: (b * n_tiles + t, 0)),
        out_shape=jax.ShapeDtypeStruct((nt, D_MODEL), F32),
        scratch_shapes=[pltpu.VMEM((tile + 2 * HALO, D_MODEL), F32),
                        pltpu.VMEM((2, tile + 2 * HALO, chunk_w), F32),
                        pltpu.VMEM((2, tile + 2 * HALO, chunk_w), F32)],
        compiler_params=_cparams(2),
        name="ffn",
    )(x1, x1, x1, wts["norm2_w"], wts["w_ffn_up"], wts["ffn_conv_w"], wts["ffn_conv_b"], wts["w_ffn_down"],
      wts["final_norm_w"])


def _block_tri(n, blk, kind):
    i = np.arange(n)[:, None]
    j = np.arange(n)[None, :]
    m = (i // blk) == (j // blk)
    if kind == "lower":
        m = m & (i >= j)
    elif kind == "upper":
        m = m & (i <= j)
    return jnp.asarray(m, dtype=BF16)


def _prep_weights(norm1_w, w_in, ssd_conv_w, ssd_conv_b, ssd_a_log, ssd_dt_bias, ssd_d, ssd_norm_w, w_ssd_out,
                  gla_gate_w2, gla_gate_b, gla_norm_w, w_gla_out, w_o, norm2_w, w_ffn_up, ffn_conv_w, ffn_conv_b,
                  w_ffn_down, final_norm_w):
    o_z = 0
    o_xbc = o_z + SSD_D_INNER
    o_dt = o_xbc + SSD_XBC
    o_q = o_dt + 2 * SSD_N_HEADS
    o_k = o_q + GLA_DK
    o_v = o_k + GLA_DK
    o_g = o_v + GLA_DV
    o_gk = o_g + GLA_DV
    o_gate = o_gk + 2 * GLA_GATE_RANK
    w_big = jnp.concatenate([w_in[:, o_xbc:o_dt], w_in[:, o_q:o_gk], w_in[:, o_z:o_xbc], w_in[:, o_gate:]],
                            axis=1).astype(BF16)
    pad = SMALL_W - 2 * SSD_N_HEADS - 2 * GLA_GATE_RANK
    w_small = jnp.concatenate([w_in[:, o_dt:o_q], w_in[:, o_gk:o_gate], jnp.zeros((D_MODEL, pad), F32)],
                              axis=1).astype(BF16)

    def row128(v):
        return jnp.concatenate([v.reshape(-1), jnp.zeros((SMALL_W - v.size,), F32)]).reshape(1, SMALL_W)

    head_of_lane = np.arange(SSD_D_INNER) // SSD_HEAD_DIM
    ef = (np.arange(SMALL_W)[:, None] == head_of_lane[None, :])
    eb = (np.arange(SMALL_W)[:, None] == (head_of_lane[None, :] + SSD_N_HEADS))
    w2 = jnp.zeros((SMALL_W, 2 * GLA_DK), F32)
    w2 = w2.at[GK_OFF:GK_OFF + GLA_GATE_RANK, :GLA_DK].set(gla_gate_w2[0])
    w2 = w2.at[GK_OFF + GLA_GATE_RANK:GK_OFF + 2 * GLA_GATE_RANK, GLA_DK:].set(gla_gate_w2[1])
    return {
        "norm1_w": norm1_w.reshape(1, D_MODEL),
        "w_big": w_big,
        "w_small": w_small,
        "ssd_conv_w": ssd_conv_w,
        "ssd_conv_b": ssd_conv_b.reshape(1, SSD_XBC),
        "ssd_bias": row128(ssd_dt_bias),
        "ssd_nega": row128(-jnp.exp(ssd_a_log)),
        "ssd_tri": _block_tri(SSD_TILE, SSD_Q, "lower"),
        "ssd_ones": _block_tri(SSD_TILE, SSD_Q, "ones"),
        "ssd_tri_full": _block_tri(PRE_TILE, PRE_TILE, "lower"),
        "ssd_ones_full": _block_tri(PRE_TILE, PRE_TILE, "ones"),
        "ssd_ef2": jnp.asarray(np.concatenate([ef, ef], axis=0), dtype=BF16),
        "ssd_eb2": jnp.asarray(np.concatenate([eb, eb], axis=0), dtype=BF16),
        "ssd_drow": jnp.repeat(ssd_d, SSD_HEAD_DIM).reshape(1, SSD_D_INNER),
        "ssd_normw": ssd_norm_w.reshape(1, SSD_D_INNER),
        "w_ssd_out": w_ssd_out.astype(BF16),
        "gla_w2": w2.astype(BF16),
        "gla_gb": gla_gate_b.reshape(1, 2 * GLA_DK),
        "gla_tril": _block_tri(SSD_TILE, GLA_CHUNK, "lower"),
        "gla_triu": _block_tri(SSD_TILE, GLA_CHUNK, "upper"),
        "gla_triu_full": _block_tri(PRE_TILE, PRE_TILE, "upper"),
        "gla_normw": gla_norm_w.reshape(1, GLA_DV),
        "w_gla_out": w_gla_out.astype(BF16),
        "w_o": w_o.astype(BF16),
        "norm2_w": norm2_w.reshape(1, D_MODEL),
        "w_ffn_up": w_ffn_up.astype(BF16),
        "ffn_conv_w": ffn_conv_w,
        "ffn_conv_b": ffn_conv_b.reshape(1, 2 * D_FF),
        "w_ffn_down": w_ffn_down.astype(BF16),
        "final_norm_w": final_norm_w.reshape(1, D_MODEL),
    }


def _trunk(x3, wts):
    batch, seq_len, _ = x3.shape
    x = x3.reshape(batch * seq_len, D_MODEL)
    big, small, ssd_sb, gla_sb = _proj_pre(x, batch, seq_len, wts)
    x1 = _mix(big, small, ssd_sb, gla_sb, x, batch, seq_len, wts)
    out = _ffn(x1, batch, seq_len, wts)
    return out.reshape(batch, seq_len, D_MODEL)


def kernel(x_prompt, x_sample, norm1_w, w_in, ssd_conv_w, ssd_conv_b, ssd_a_log, ssd_dt_bias, ssd_d, ssd_norm_w,
           w_ssd_out, gla_gate_w2, gla_gate_b, gla_norm_w, w_gla_out, w_o, norm2_w, w_ffn_up, ffn_conv_w,
           ffn_conv_b, w_ffn_down, final_norm_w):
    wts = _prep_weights(norm1_w[0], w_in[0], ssd_conv_w[0], ssd_conv_b[0], ssd_a_log[0], ssd_dt_bias[0], ssd_d[0],
                        ssd_norm_w[0], w_ssd_out[0], gla_gate_w2[0], gla_gate_b[0], gla_norm_w[0], w_gla_out[0],
                        w_o[0], norm2_w[0], w_ffn_up[0], ffn_conv_w[0], ffn_conv_b[0], w_ffn_down[0], final_norm_w)
    return (_trunk(x_prompt, wts), _trunk(x_sample, wts))
```

```python
import functools

import numpy as np
import jax
import jax.numpy as jnp
from jax import lax
from jax.experimental import pallas as pl
from jax.experimental.pallas import tpu as pltpu

F32 = jnp.float32
BF16 = jnp.bfloat16

D_MODEL = 1024
SSD_D_INNER = 2048
SSD_HEAD_DIM = 64
SSD_N_HEADS = 32
SSD_N_GROUPS = 4
SSD_D_STATE = 128
SSD_GROUP_W = SSD_D_INNER // SSD_N_GROUPS
SSD_BC_W = SSD_N_GROUPS * SSD_D_STATE
SSD_XBC = SSD_D_INNER + 2 * SSD_BC_W
SSD_CONV = 5
GLA_N_HEADS = 4
GLA_DK = 512
GLA_DV = 1024
GLA_HEAD_K = 128
GLA_HEAD_V = 256
GLA_GATE_RANK = 16
GLA_GATE_NORM = 16.0
GLA_CHUNK = 64
D_FF = 2816
EPS = 1e-6
LOG2E = 1.4426950408889634

LANES = 128
SUBLANES = 8
VMEM_LIMIT = 56 * 1024 * 1024

BIG_W = SSD_XBC + 2 * GLA_DK + 2 * GLA_DV + SSD_D_INNER + 2 * D_MODEL
SMALL_W = LANES
GK_OFF = 2 * SSD_N_HEADS

PRE_TILE = 256
SSD_TILE = 256
SSD_Q = 64
assert PRE_TILE == SSD_TILE
assert SSD_Q == GLA_CHUNK
HALO = SUBLANES
SSD_HALO = 2 * SUBLANES
CONV_SLAB = 256
FFN_TILE = 512
FFN_CHUNKS = ((0, 1024), (1024, 2048), (2048, D_FF))
REST_TN = 1024


def _dot(a, b):
    return jnp.dot(a, b, preferred_element_type=F32)


def _dot_tn(a, b):
    return lax.dot_general(a, b, (((0,), (0,)), ((), ())), preferred_element_type=F32)


def _dot_nt(a, b):
    return lax.dot_general(a, b, (((1,), (1,)), ((), ())), preferred_element_type=F32)


def _split_bf16(x, n):
    parts = []
    r = x
    for i in range(n):
        p = r.astype(BF16)
        parts.append(p)
        if i + 1 < n:
            r = r - p.astype(F32)
    return parts


def _dot_exact_lhs(m_bf16, x, n=3):
    acc = None
    for p in _split_bf16(x, n):
        t = _dot(m_bf16, p)
        acc = t if acc is None else acc + t
    return acc


def _sigmoid(x):
    return 0.5 * jnp.tanh(0.5 * x) + 0.5


def _silu(x):
    h = 0.5 * x
    return h + h * jnp.tanh(h)


def _softplus(x):
    return jnp.maximum(x, 0.0) + jnp.log1p(jnp.exp(-jnp.abs(x)))


def _cparams(n_axes):
    return pltpu.CompilerParams(dimension_semantics=("arbitrary",) * n_axes,
                                vmem_limit_bytes=VMEM_LIMIT)


def _proj_pre_kernel(xprev_ref, xcur_ref, xnext_ref, nw_ref, wbig_ref, wsmall_ref,
                     convw_ref, convb_ref, bias_ref, nega_ref, tri_ref, ones_ref, eb2_ref,
                     w2_ref, gb_ref, triu_ref,
                     big_ref, small_ref, ssd_sbin_ref, gla_sbin_ref,
                     h_ref, ext_ref, ssd_sb_ref, gla_sb_ref, *, n_tiles):
    t = pl.program_id(1)
    tile = PRE_TILE
    halo = SSD_HALO
    first = t == n_tiles - 1
    last = t == 0

    @pl.when(t == 0)
    def _():
        ssd_sb_ref[...] = jnp.zeros_like(ssd_sb_ref)
        gla_sb_ref[...] = jnp.zeros_like(gla_sb_ref)

    ssd_sbin_ref[0, 0] = ssd_sb_ref[...]
    gla_sbin_ref[0, 0] = gla_sb_ref[...]

    def norm_rows(x):
        ms = jnp.mean(x * x, axis=-1, keepdims=True)
        return (x * lax.rsqrt(ms + EPS) * nw_ref[...]).astype(BF16)

    h_ref[0:halo, :] = norm_rows(xprev_ref[...])
    h_ref[halo:halo + tile, :] = norm_rows(xcur_ref[...])
    h_ref[halo + tile:halo + tile + halo, :] = norm_rows(xnext_ref[...])

    row = lax.broadcasted_iota(jnp.int32, (tile + 2 * halo, 1), 0)
    keep = jnp.logical_and(jnp.logical_or(row >= halo, jnp.logical_not(first)),
                           jnp.logical_or(row < halo + tile, jnp.logical_not(last)))
    h = h_ref[halo:halo + tile, :]

    def xbc_slab(lo):
        ext_ref[(lo // CONV_SLAB) % 2] = jnp.where(keep, _dot(h_ref[...], wbig_ref[:, lo:lo + CONV_SLAB]), 0.0)

    def conv_slab(lo):
        ext = ext_ref.at[(lo // CONV_SLAB) % 2]
        y = _conv_taps(ext, convw_ref, convb_ref, halo, tile, lo, lo + CONV_SLAB)
        big_ref[:, lo:lo + CONV_SLAB] = _silu(y).astype(BF16)

    def rest_chunk(lo):
        big_ref[:, lo:lo + REST_TN] = _dot(h, wbig_ref[:, lo:lo + REST_TN]).astype(BF16)

    slabs = list(range(0, SSD_XBC, CONV_SLAB))
    rest = list(range(SSD_XBC, BIG_W, REST_TN))
    xbc_slab(slabs[0])
    for i, lo in enumerate(slabs):
        if i + 1 < len(slabs):
            xbc_slab(slabs[i + 1])
        if rest:
            rest_chunk(rest.pop(0))
        conv_slab(lo)
    for lo in rest:
        rest_chunk(lo)
    small_ref[...] = _dot(h, wsmall_ref[...])

    pr, w, _ = _ssd_small(small_ref, bias_ref, nega_ref, tri_ref, ones_ref)
    wb = _dot(w.astype(BF16), eb2_ref[0:LANES, :]).astype(BF16)
    decb = _expand2(jnp.exp2(pr[0:SUBLANES, :]), eb2_ref)[0:1, :]
    sb = [ssd_sb_ref[g] for g in range(SSD_N_GROUPS)]
    new = _ssd_bwd_state_step(sb, big_ref[:, 0:SSD_D_INNER],
                              big_ref[:, SSD_D_INNER:SSD_D_INNER + SSD_BC_W], wb, decb)
    for g in range(SSD_N_GROUPS):
        ssd_sb_ref[g] = new[g]

    la = _gla_log_alpha(small_ref, w2_ref[:, GLA_DK:], gb_ref[:, GLA_DK:])
    cs = _dot_exact_lhs(triu_ref[...], la, 2)
    k_off = SSD_XBC + GLA_DK
    v_off = SSD_XBC + 2 * GLA_DK
    for hd in range(GLA_N_HEADS):
        k_h = big_ref[:, k_off + hd * GLA_HEAD_K:k_off + (hd + 1) * GLA_HEAD_K].astype(F32)
        v_h = big_ref[:, v_off + hd * GLA_HEAD_V:v_off + (hd + 1) * GLA_HEAD_V]
        gla_sb_ref[hd] = _gla_bwd_state_step(gla_sb_ref[hd], k_h, v_h,
                                             cs[:, hd * GLA_HEAD_K:(hd + 1) * GLA_HEAD_K])


def _proj_pre(x, batch, seq_len, wts):
    nt = batch * seq_len
    tile = PRE_TILE
    n_tiles = seq_len // tile
    rb = tile // SSD_HALO
    seq_rb = seq_len // SSD_HALO
    last_rb = nt // SSD_HALO - 1
    rev = lambda t: n_tiles - 1 - t
    ssd_state = (SSD_N_GROUPS, SSD_D_STATE, SSD_GROUP_W)
    gla_state = (GLA_N_HEADS, GLA_HEAD_V, GLA_HEAD_K)
    consts = [wts["ssd_conv_w"], wts["ssd_conv_b"], wts["ssd_bias"], wts["ssd_nega"],
              wts["ssd_tri_full"], wts["ssd_ones_full"], wts["ssd_eb2"],
              wts["gla_w2"], wts["gla_gb"], wts["gla_triu_full"]]
    row = lambda b, t: (b * n_tiles + rev(t), 0)
    return pl.pallas_call(
        functools.partial(_proj_pre_kernel, n_tiles=n_tiles),
        grid=(batch, n_tiles),
        in_specs=[pl.BlockSpec((SSD_HALO, D_MODEL),
                               lambda b, t: (jnp.maximum(b * seq_rb + rev(t) * rb - 1, 0), 0)),
                  pl.BlockSpec((tile, D_MODEL), row),
                  pl.BlockSpec((SSD_HALO, D_MODEL),
                               lambda b, t: (jnp.minimum(b * seq_rb + (rev(t) + 1) * rb, last_rb), 0)),
                  pl.BlockSpec((1, D_MODEL), lambda b, t: (0, 0)),
                  pl.BlockSpec((D_MODEL, BIG_W), lambda b, t: (0, 0), pipeline_mode=pl.Buffered(1)),
                  pl.BlockSpec((D_MODEL, SMALL_W), lambda b, t: (0, 0))]
        + [_const_spec(c) for c in consts],
        out_specs=[pl.BlockSpec((tile, BIG_W), row),
                   pl.BlockSpec((tile, SMALL_W), row),
                   pl.BlockSpec((1, 1) + ssd_state, lambda b, t: (b, rev(t), 0, 0, 0)),
                   pl.BlockSpec((1, 1) + gla_state, lambda b, t: (b, rev(t), 0, 0, 0))],
        out_shape=[jax.ShapeDtypeStruct((nt, BIG_W), BF16),
                   jax.ShapeDtypeStruct((nt, SMALL_W), F32),
                   jax.ShapeDtypeStruct((batch, n_tiles) + ssd_state, F32),
                   jax.ShapeDtypeStruct((batch, n_tiles) + gla_state, F32)],
        scratch_shapes=[pltpu.VMEM((tile + 2 * SSD_HALO, D_MODEL), BF16),
                        pltpu.VMEM((2, tile + 2 * SSD_HALO, CONV_SLAB), F32),
                        pltpu.VMEM(ssd_state, F32),
                        pltpu.VMEM(gla_state, F32)],
        compiler_params=_cparams(2),
        name="proj_pre",
    )(x, x, x, wts["norm1_w"], wts["w_big"], wts["w_small"], *consts)


def _conv_taps(ext_ref, w_ref, b_ref, row0, n_rows, lo, hi):
    pad = (SSD_CONV - 1) // 2
    acc = b_ref[:, lo:hi]
    for k in range(SSD_CONV):
        acc = acc + ext_ref[row0 - pad + k:row0 - pad + k + n_rows, :] * w_ref[k:k + 1, lo:hi]
    return acc


def _ssd_small(small_ref, bias_ref, nega_ref, tri_ref, ones_ref):
    s = small_ref[...]
    dt = _softplus(s + bias_ref[...])
    a = dt * nega_ref[...]
    p = _dot_exact_lhs(tri_ref[...], a)
    tot = _dot_exact_lhs(ones_ref[...], a)
    col = lax.broadcasted_iota(jnp.int32, s.shape, 1)
    fwd = col < SSD_N_HEADS
    r = tot - p + a
    pr = jnp.where(fwd, p, r)
    w = dt * jnp.exp2(jnp.where(fwd, tot - p, p - a))
    rowarg = jnp.log2(dt) - pr
    dsum = pltpu.roll(dt, 2 * SSD_N_HEADS, 1) + pltpu.roll(dt, SSD_N_HEADS, 1)
    diag_cols = (col >= 2 * SSD_N_HEADS) & (col < 3 * SSD_N_HEADS)
    rowarg = jnp.where(diag_cols, jnp.log2(dsum), rowarg)
    return pr, w, rowarg


def _expand2(x, e2_ref):
    hi, lo = _split_bf16(x, 2)
    return _dot(jnp.concatenate([hi, lo], axis=1), e2_ref[...])


def _ssd_bwd_state_step(sb, xs_c, b_c, wb_c, decb_row):
    out = []
    for g in range(SSD_N_GROUPS):
        gl = slice(g * SSD_GROUP_W, (g + 1) * SSD_GROUP_W)
        xd = xs_c[:, gl] * wb_c[:, gl]
        bg = b_c[:, g * SSD_D_STATE:(g + 1) * SSD_D_STATE]
        out.append(decb_row[:, gl] * sb[g] + _dot_tn(bg, xd))
    return out


def _const_spec(arr):
    nd = arr.ndim
    return pl.BlockSpec(arr.shape, lambda b, t: (0,) * nd)


def _gla_log_alpha(small_ref, w2, gb):
    pre = _dot(small_ref[...].astype(BF16), w2) + gb
    log_sig = -(jnp.maximum(-pre, 0.0) + jnp.log(1.0 + jnp.exp(-jnp.abs(pre))))
    return log_sig * (LOG2E / GLA_GATE_NORM)


def _gla_bwd_state_step(sb_h, k_h, v_h, cs_h):
    tot = cs_h[0:1, :]
    k_dec = (k_h * jnp.exp2(tot - cs_h)).astype(BF16)
    return jnp.exp2(tot) * sb_h + _dot_tn(v_h, k_dec)


def _mix_kernel(xc_ref, qkvg_ref, z_ref, gates_ref, small_ref, ssd_sbin_ref, gla_sbin_ref, x_ref,
                bias_ref, nega_ref, tri_ref, ones_ref, ef2_ref, eb2_ref, drow_ref, snormw_ref, swout_ref,
                w2_ref, gb_ref, tril_ref, triu_ref, gnormw_ref, gwout_ref, wo_ref,
                o_ref,
                rowarg_ref, prf_ref, prb_ref, wf_ref, wb_ref, exf_ref, exb_ref, sf_ref, sbc_ref, y_ref,
                cs_ref, gsf_ref, gsbc_ref, oacc_ref):
    t = pl.program_id(1)
    tile = SSD_TILE
    q = SSD_Q
    nc = tile // q
    b_off = SSD_D_INNER
    c_off = SSD_D_INNER + SSD_BC_W
    scale = GLA_HEAD_K ** -0.5
    hpg = SSD_N_HEADS // SSD_N_GROUPS

    @pl.when(t == 0)
    def _():
        sf_ref[...] = jnp.zeros_like(sf_ref)
        gsf_ref[...] = jnp.zeros_like(gsf_ref)

    def ssd_prep():
        pr, w, rowarg = _ssd_small(small_ref, bias_ref, nega_ref, tri_ref, ones_ref)
        rowarg_ref[...] = rowarg
        wbf = w.astype(BF16)
        wf_ref[...] = _dot(wbf, ef2_ref[0:LANES, :]).astype(BF16)
        wb_ref[...] = _dot(wbf, eb2_ref[0:LANES, :]).astype(BF16)
        prf = _expand2(pr, ef2_ref)
        prf_ref[...] = prf
        exf_ref[...] = jnp.exp2(prf)
        prb = _expand2(pr, eb2_ref)
        prb_ref[...] = prb
        exb_ref[...] = jnp.exp2(prb)
        sbc_ref[nc - 1] = ssd_sbin_ref[0, 0]

    def ssd_bwd(c):
        r0 = c * q
        rows = slice(r0, r0 + q)
        sb = [sbc_ref[c, g] for g in range(SSD_N_GROUPS)]
        new = _ssd_bwd_state_step(sb, xc_ref[rows, 0:SSD_D_INNER], xc_ref[rows, b_off:c_off],
                                  wb_ref[rows, :], exb_ref[r0:r0 + 1, :])
        for g in range(SSD_N_GROUPS):
            sbc_ref[c - 1, g] = new[g]

    ii = lax.broadcasted_iota(jnp.int32, (q, LANES), 0)
    ll = lax.broadcasted_iota(jnp.int32, (q, LANES), 1)
    jj = jnp.where(ll < SSD_HEAD_DIM, ll, ll - SSD_HEAD_DIM)
    lower = ii > jj
    upper = ii < jj
    left = ll < SSD_HEAD_DIM
    left_row = lax.broadcasted_iota(jnp.int32, (1, LANES), 1) < SSD_HEAD_DIM

    def ssd_chunk(c):
        k0 = (c // 2) * 2 * q
        s = c % 2
        r0 = c * q
        rows = slice(r0, r0 + q)
        row_t = rowarg_ref[k0:k0 + 2 * q, :].T
        row_tr = pltpu.roll(row_t, q, 1)
        lo_src, hi_src = (row_t, row_tr) if s == 0 else (row_tr, row_t)

        def pair_row(base, h0):
            return jnp.where(left_row, lo_src[base + h0:base + h0 + 1, :],
                             hi_src[base + h0 + 1:base + h0 + 2, :])

        for g in range(SSD_N_GROUPS):
            gl = slice(g * SSD_GROUP_W, (g + 1) * SSD_GROUP_W)
            b_g = xc_ref[rows, b_off + g * SSD_D_STATE:b_off + (g + 1) * SSD_D_STATE]
            c_g = xc_ref[rows, c_off + g * SSD_D_STATE:c_off + (g + 1) * SSD_D_STATE]
            cb2 = _dot_nt(c_g, jnp.concatenate([b_g, b_g], axis=0))
            sf = sf_ref[g]
            y_off = (_dot(c_g, sf.astype(BF16)) * exf_ref[rows, gl]
                     + _dot(c_g, sbc_ref[c, g].astype(BF16)) * exb_ref[rows, gl])
            xs_g = xc_ref[rows, gl].astype(F32)
            for pm in range(hpg // 2):
                h0 = g * hpg + 2 * pm
                cl = slice(g * SSD_GROUP_W + pm * LANES, g * SSD_GROUP_W + (pm + 1) * LANES)
                arg = jnp.where(
                    lower, prf_ref[rows, cl] + pair_row(0, h0),
                    jnp.where(upper, prb_ref[rows, cl] + pair_row(SSD_N_HEADS, h0),
                              pair_row(2 * SSD_N_HEADS, h0)))
                m2 = (cb2 * jnp.exp2(arg)).astype(BF16)
                x_pair = xs_g[:, pm * LANES:(pm + 1) * LANES]
                x2 = jnp.concatenate([jnp.where(left, x_pair, 0.0), jnp.where(left, 0.0, x_pair)],
                                     axis=0).astype(BF16)
                y_ref[rows, cl] = (_dot(m2, x2) + y_off[:, pm * LANES:(pm + 1) * LANES]
                                   + drow_ref[:, cl] * x_pair)
            xd = xc_ref[rows, gl] * wf_ref[rows, gl]
            decf = exf_ref[r0 + q - 1:r0 + q, gl]
            sf_ref[g] = decf * sf + _dot_tn(b_g, xd)

    def ssd_epilogue():
        z = z_ref[...].astype(F32)
        y = y_ref[...] * _silu(z)
        outs = []
        for g in range(SSD_N_GROUPS):
            gl = slice(g * SSD_GROUP_W, (g + 1) * SSD_GROUP_W)
            yg = y[:, gl]
            ms = jnp.mean(yg * yg, axis=-1, keepdims=True)
            outs.append((yg * lax.rsqrt(ms + EPS) * snormw_ref[:, gl]).astype(BF16))
        branch_a = _dot(jnp.concatenate(outs, axis=1), swout_ref[...])
        return _sigmoid(gates_ref[:, 0:D_MODEL].astype(F32)) * branch_a

    def gla_prep():
        la = _gla_log_alpha(small_ref, w2_ref[...], gb_ref[...])
        cs_ref[:, 0:GLA_DK] = _dot_exact_lhs(tril_ref[...], la[:, 0:GLA_DK], 2)
        cs_ref[:, GLA_DK:2 * GLA_DK] = _dot_exact_lhs(triu_ref[...], la[:, GLA_DK:2 * GLA_DK], 2)
        gsbc_ref[nc - 1] = gla_sbin_ref[0, 0]

    def gla_bwd(c):
        rows = slice(c * q, (c + 1) * q)
        for h in range(GLA_N_HEADS):
            kl = slice(GLA_DK + h * GLA_HEAD_K, GLA_DK + (h + 1) * GLA_HEAD_K)
            k_h = qkvg_ref[rows, kl].astype(F32)
            v_h = qkvg_ref[rows, 2 * GLA_DK + h * GLA_HEAD_V:2 * GLA_DK + (h + 1) * GLA_HEAD_V]
            gsbc_ref[c - 1, h] = _gla_bwd_state_step(gsbc_ref[c, h], k_h, v_h, cs_ref[rows, kl])

    gi = lax.broadcasted_iota(jnp.int32, (q, q), 0)
    gj = lax.broadcasted_iota(jnp.int32, (q, q), 1)
    lower_inc = gi >= gj
    upper_inc = gi <= gj

    def gla_chunk(c):
        rows = slice(c * q, (c + 1) * q)
        for h in range(GLA_N_HEADS):
            fl = slice(h * GLA_HEAD_K, (h + 1) * GLA_HEAD_K)
            bl = slice(GLA_DK + h * GLA_HEAD_K, GLA_DK + (h + 1) * GLA_HEAD_K)
            q_h = qkvg_ref[rows, fl].astype(F32) * scale
            k_h = qkvg_ref[rows, bl].astype(F32)
            v_h = qkvg_ref[rows, 2 * GLA_DK + h * GLA_HEAD_V:2 * GLA_DK + (h + 1) * GLA_HEAD_V]
            csf = cs_ref[rows, fl]
            csb = cs_ref[rows, bl]
            qf = (q_h * jnp.exp2(csf)).astype(BF16)
            kf = (k_h * jnp.exp2(-csf)).astype(BF16)
            qb = (q_h * jnp.exp2(csb)).astype(BF16)
            kb = (k_h * jnp.exp2(-csb)).astype(BF16)
            att = (jnp.where(lower_inc, _dot_nt(qf, kf), 0.0)
                   + jnp.where(upper_inc, _dot_nt(qb, kb), 0.0))
            sf = gsf_ref[h]
            s_cat = jnp.concatenate([sf, gsbc_ref[c, h]], axis=1).astype(BF16)
            o = _dot(att.astype(BF16), v_h) + _dot_nt(jnp.concatenate([qf, qb], axis=1), s_cat)
            oacc_ref[rows, h * GLA_HEAD_V:(h + 1) * GLA_HEAD_V] = o
            tot = csf[q - 1:q, :]
            k_dec = (k_h * jnp.exp2(tot - csf)).astype(BF16)
            gsf_ref[h] = jnp.exp2(tot) * sf + _dot_tn(v_h, k_dec)

    def gla_epilogue(a_gated):
        g = qkvg_ref[:, 2 * GLA_DK + GLA_DV:2 * GLA_DK + 2 * GLA_DV].astype(F32)
        o = oacc_ref[...]
        outs = []
        for h in range(GLA_N_HEADS):
            hl = slice(h * GLA_HEAD_V, (h + 1) * GLA_HEAD_V)
            oh = o[:, hl]
            ms = jnp.mean(oh * oh, axis=-1, keepdims=True)
            outs.append((oh * lax.rsqrt(ms + EPS) * gnormw_ref[:, hl] * _silu(g[:, hl])).astype(BF16))
        branch_b = _dot(jnp.concatenate(outs, axis=1), gwout_ref[...])
        mixed = a_gated + _sigmoid(gates_ref[:, D_MODEL:2 * D_MODEL].astype(F32)) * branch_b
        o_ref[...] = x_ref[...] + _dot(mixed.astype(BF16), wo_ref[...])

    ssd_prep()
    gla_prep()
    for c in range(nc - 1, 0, -1):
        ssd_bwd(c)
        gla_bwd(c)
    for c in range(nc):
        ssd_chunk(c)
        gla_chunk(c)
    gla_epilogue(ssd_epilogue())


def _mix(big, small, ssd_sb, gla_sb, x, batch, seq_len, wts):
    nt = batch * seq_len
    tile = SSD_TILE
    n_tiles = seq_len // tile
    nc = tile // SSD_Q
    ssd_state = (SSD_N_GROUPS, SSD_D_STATE, SSD_GROUP_W)
    gla_state = (GLA_N_HEADS, GLA_HEAD_V, GLA_HEAD_K)
    consts = [wts["ssd_bias"], wts["ssd_nega"], wts["ssd_tri"], wts["ssd_ones"], wts["ssd_ef2"], wts["ssd_eb2"],
              wts["ssd_drow"], wts["ssd_normw"], wts["w_ssd_out"],
              wts["gla_w2"], wts["gla_gb"], wts["gla_tril"], wts["gla_triu"], wts["gla_normw"],
              wts["w_gla_out"], wts["w_o"]]
    qkvg_w = 2 * GLA_DK + 2 * GLA_DV
    z_block = (SSD_XBC + qkvg_w) // SSD_D_INNER
    gates_block = (BIG_W - 2 * D_MODEL) // (2 * D_MODEL)
    row = lambda b, t: (b * n_tiles + t, 0)
    wide = pltpu.VMEM((tile, SSD_D_INNER), F32)
    wide16 = pltpu.VMEM((tile, SSD_D_INNER), BF16)
    return pl.pallas_call(
        _mix_kernel,
        grid=(batch, n_tiles),
        in_specs=[pl.BlockSpec((tile, SSD_XBC), row),
                  pl.BlockSpec((tile, qkvg_w), lambda b, t: (b * n_tiles + t, SSD_XBC // qkvg_w)),
                  pl.BlockSpec((tile, SSD_D_INNER), lambda b, t: (b * n_tiles + t, z_block)),
                  pl.BlockSpec((tile, 2 * D_MODEL), lambda b, t: (b * n_tiles + t, gates_block)),
                  pl.BlockSpec((tile, SMALL_W), row),
                  pl.BlockSpec((1, 1) + ssd_state, lambda b, t: (b, t, 0, 0, 0)),
                  pl.BlockSpec((1, 1) + gla_state, lambda b, t: (b, t, 0, 0, 0)),
                  pl.BlockSpec((tile, D_MODEL), row)]
        + [_const_spec(c) for c in consts],
        out_specs=pl.BlockSpec((tile, D_MODEL), row),
        out_shape=jax.ShapeDtypeStruct((nt, D_MODEL), F32),
        scratch_shapes=[pltpu.VMEM((tile, SMALL_W), F32),
                        wide, wide, wide16, wide16, wide, wide,
                        pltpu.VMEM(ssd_state, F32),
                        pltpu.VMEM((nc,) + ssd_state, F32),
                        wide,
                        pltpu.VMEM((tile, 2 * GLA_DK), F32),
                        pltpu.VMEM(gla_state, F32),
                        pltpu.VMEM((nc,) + gla_state, F32),
                        pltpu.VMEM((tile, GLA_DV), F32)],
        compiler_params=_cparams(2),
        name="mix",
    )(big, big, big, big, small, ssd_sb, gla_sb, x, *consts)


def _ffn_kernel(prev_ref, cur_ref, next_ref, n2_ref, wup_ref, cw_ref, cb_ref, wd_ref, fn_ref, o_ref,
                h_ref, ug_ref, uv_ref, *, n_tiles):
    t = pl.program_id(1)
    tile = FFN_TILE

    def norm_rows(x):
        ms = jnp.mean(x * x, axis=-1, keepdims=True)
        return x * lax.rsqrt(ms + EPS) * n2_ref[...]

    h_ref[0:HALO, :] = norm_rows(prev_ref[...])
    h_ref[HALO:HALO + tile, :] = norm_rows(cur_ref[...])
    h_ref[HALO + tile:HALO + tile + HALO, :] = norm_rows(next_ref[...])
    h = h_ref[...].astype(BF16)

    row = lax.broadcasted_iota(jnp.int32, (tile + 2 * HALO, 1), 0)
    keep = jnp.logical_and(jnp.logical_or(row >= HALO, t > 0),
                           jnp.logical_or(row < HALO + tile, t < n_tiles - 1))

    def up(ci):
        lo, hi = FFN_CHUNKS[ci]
        ug_ref[ci % 2, :, 0:hi - lo] = jnp.where(keep, _dot(h, wup_ref[:, lo:hi]), 0.0)
        uv_ref[ci % 2, :, 0:hi - lo] = jnp.where(keep, _dot(h, wup_ref[:, D_FF + lo:D_FF + hi]), 0.0)

    def conv3(u_ref, ci, col0):
        lo, hi = FFN_CHUNKS[ci]
        acc = cb_ref[:, col0 + lo:col0 + hi]
        for k in range(3):
            u_k = u_ref[ci % 2, HALO - 1 + k:HALO - 1 + k + tile, 0:hi - lo]
            acc = acc + u_k * cw_ref[k:k + 1, col0 + lo:col0 + hi]
        return acc

    def down(ci):
        lo, hi = FFN_CHUNKS[ci]
        act = _silu(conv3(ug_ref, ci, 0)) * conv3(uv_ref, ci, D_FF)
        return _dot(act.astype(BF16), wd_ref[lo:hi, :])

    up(0)
    acc = None
    for ci in range(len(FFN_CHUNKS)):
        if ci + 1 < len(FFN_CHUNKS):
            up(ci + 1)
        d = down(ci)
        acc = d if acc is None else acc + d

    x2 = cur_ref[...] + acc
    ms = jnp.mean(x2 * x2, axis=-1, keepdims=True)
    o_ref[...] = x2 * lax.rsqrt(ms + EPS) * fn_ref[...]


def _ffn(x1, batch, seq_len, wts):
    nt = batch * seq_len
    tile = FFN_TILE
    n_tiles = seq_len // tile
    rb = tile // HALO
    seq_rb = seq_len // HALO
    last_rb = nt // HALO - 1
    c2 = lambda b, t: (0, 0)
    chunk_w = max(hi - lo for lo, hi in FFN_CHUNKS)
    resident = pl.Buffered(1)
    return pl.pallas_call(
        functools.partial(_ffn_kernel, n_tiles=n_tiles),
        grid=(batch, n_tiles),
        in_specs=[pl.BlockSpec((HALO, D_MODEL), lambda b, t: (jnp.maximum(b * seq_rb + t * rb - 1, 0), 0)),
                  pl.BlockSpec((tile, D_MODEL), lambda b, t: (b * n_tiles + t, 0)),
                  pl.BlockSpec((HALO, D_MODEL),
                               lambda b, t: (jnp.minimum(b * seq_rb + (t + 1) * rb, last_rb), 0)),
                  pl.BlockSpec((1, D_MODEL), c2),
                  pl.BlockSpec((D_MODEL, 2 * D_FF), c2, pipeline_mode=resident),
                  pl.BlockSpec((3, 2 * D_FF), c2),
                  pl.BlockSpec((1, 2 * D_FF), c2),
                  pl.BlockSpec((D_FF, D_MODEL), c2, pipeline_mode=resident),
                  pl.BlockSpec((1, D_MODEL), c2)],
        out_specs=pl.BlockSpec((tile, D_MODEL), lambda b, t: (b * n_tiles + t, 0)),
        out_shape=jax.ShapeDtypeStruct((nt, D_MODEL), F32),
        scratch_shapes=[pltpu.VMEM((tile + 2 * HALO, D_MODEL), F32),
                        pltpu.VMEM((2, tile + 2 * HALO, chunk_w), F32),
                        pltpu.VMEM((2, tile + 2 * HALO, chunk_w), F32)],
        compiler_params=_cparams(2),
        name="ffn",
    )(x1, x1, x1, wts["norm2_w"], wts["w_ffn_up"], wts["ffn_conv_w"], wts["ffn_conv_b"], wts["w_ffn_down"],
      wts["final_norm_w"])


def _block_tri(n, blk, kind):
    i = np.arange(n)[:, None]
    j = np.arange(n)[None, :]
    m = (i // blk) == (j // blk)
    if kind == "lower":
        m = m & (i >= j)
    elif kind == "upper":
        m = m & (i <= j)
    return jnp.asarray(m, dtype=BF16)


def _prep_weights(norm1_w, w_in, ssd_conv_w, ssd_conv_b, ssd_a_log, ssd_dt_bias, ssd_d, ssd_norm_w, w_ssd_out,
                  gla_gate_w2, gla_gate_b, gla_norm_w, w_gla_out, w_o, norm2_w, w_ffn_up, ffn_conv_w, ffn_conv_b,
                  w_ffn_down, final_norm_w):
    o_z = 0
    o_xbc = o_z + SSD_D_INNER
    o_dt = o_xbc + SSD_XBC
    o_q = o_dt + 2 * SSD_N_HEADS
    o_k = o_q + GLA_DK
    o_v = o_k + GLA_DK
    o_g = o_v + GLA_DV
    o_gk = o_g + GLA_DV
    o_gate = o_gk + 2 * GLA_GATE_RANK
    w_big = jnp.concatenate([w_in[:, o_xbc:o_dt], w_in[:, o_q:o_gk], w_in[:, o_z:o_xbc], w_in[:, o_gate:]],
                            axis=1).astype(BF16)
    pad = SMALL_W - 2 * SSD_N_HEADS - 2 * GLA_GATE_RANK
    w_small = jnp.concatenate([w_in[:, o_dt:o_q], w_in[:, o_gk:o_gate], jnp.zeros((D_MODEL, pad), F32)],
                              axis=1).astype(BF16)

    def row128(v):
        return jnp.concatenate([v.reshape(-1), jnp.zeros((SMALL_W - v.size,), F32)]).reshape(1, SMALL_W)

    head_of_lane = np.arange(SSD_D_INNER) // SSD_HEAD_DIM
    ef = (np.arange(SMALL_W)[:, None] == head_of_lane[None, :])
    eb = (np.arange(SMALL_W)[:, None] == (head_of_lane[None, :] + SSD_N_HEADS))
    w2 = jnp.zeros((SMALL_W, 2 * GLA_DK), F32)
    w2 = w2.at[GK_OFF:GK_OFF + GLA_GATE_RANK, :GLA_DK].set(gla_gate_w2[0])
    w2 = w2.at[GK_OFF + GLA_GATE_RANK:GK_OFF + 2 * GLA_GATE_RANK, GLA_DK:].set(gla_gate_w2[1])
    return {
        "norm1_w": norm1_w.reshape(1, D_MODEL),
        "w_big": w_big,
        "w_small": w_small,
        "ssd_conv_w": ssd_conv_w,
        "ssd_conv_b": ssd_conv_b.reshape(1, SSD_XBC),
        "ssd_bias": row128(ssd_dt_bias),
        "ssd_nega": row128(-jnp.exp(ssd_a_log) * LOG2E),
        "ssd_tri": _block_tri(SSD_TILE, SSD_Q, "lower"),
        "ssd_ones": _block_tri(SSD_TILE, SSD_Q, "ones"),
        "ssd_tri_full": _block_tri(PRE_TILE, PRE_TILE, "lower"),
        "ssd_ones_full": _block_tri(PRE_TILE, PRE_TILE, "ones"),
        "ssd_ef2": jnp.asarray(np.concatenate([ef, ef], axis=0), dtype=BF16),
        "ssd_eb2": jnp.asarray(np.concatenate([eb, eb], axis=0), dtype=BF16),
        "ssd_drow": jnp.repeat(ssd_d, SSD_HEAD_DIM).reshape(1, SSD_D_INNER),
        "ssd_normw": ssd_norm_w.reshape(1, SSD_D_INNER),
        "w_ssd_out": w_ssd_out.astype(BF16),
        "gla_w2": w2.astype(BF16),
        "gla_gb": gla_gate_b.reshape(1, 2 * GLA_DK),
        "gla_tril": _block_tri(SSD_TILE, GLA_CHUNK, "lower"),
        "gla_triu": _block_tri(SSD_TILE, GLA_CHUNK, "upper"),
        "gla_triu_full": _block_tri(PRE_TILE, PRE_TILE, "upper"),
        "gla_normw": gla_norm_w.reshape(1, GLA_DV),
        "w_gla_out": w_gla_out.astype(BF16),
        "w_o": w_o.astype(BF16),
        "norm2_w": norm2_w.reshape(1, D_MODEL),
        "w_ffn_up": w_ffn_up.astype(BF16),
        "ffn_conv_w": ffn_conv_w,
        "ffn_conv_b": ffn_conv_b.reshape(1, 2 * D_FF),
        "w_ffn_down": w_ffn_down.astype(BF16),
        "final_norm_w": final_norm_w.reshape(1, D_MODEL),
    }


def _trunk(x3, wts):
    batch, seq_len, _ = x3.shape
    x = x3.reshape(batch * seq_len, D_MODEL)
    big, small, ssd_sb, gla_sb = _proj_pre(x, batch, seq_len, wts)
    x1 = _mix(big, small, ssd_sb, gla_sb, x, batch, seq_len, wts)
    out = _ffn(x1, batch, seq_len, wts)
    return out.reshape(batch, seq_len, D_MODEL)


def kernel(x_prompt, x_sample, norm1_w, w_in, ssd_conv_w, ssd_conv_b, ssd_a_log, ssd_dt_bias, ssd_d, ssd_norm_w,
           w_ssd_out, gla_gate_w2, gla_gate_b, gla_norm_w, w_gla_out, w_o, norm2_w, w_ffn_up, ffn_conv_w,
           ffn_conv_b, w_ffn_down, final_norm_w):
    wts = _prep_weights(norm1_w[0], w_in[0], ssd_conv_w[0], ssd_conv_b[0], ssd_a_log[0], ssd_dt_bias[0], ssd_d[0],
                        ssd_norm_w[0], w_ssd_out[0], gla_gate_w2[0], gla_gate_b[0], gla_norm_w[0], w_gla_out[0],
                        w_o[0], norm2_w[0], w_ffn_up[0], ffn_conv_w[0], ffn_conv_b[0], w_ffn_down[0], final_norm_w)
    return (_trunk(x_prompt, wts), _trunk(x_sample, wts))
```
